```python
import math, functools
import jax, jax.numpy as jnp
from jax import lax
import numpy as np

D_MODEL = 1024
BATCH = 4
SEQ = 8192
DEPTH = 2
DEC_BATCH = 128
DEC_SEQ = 4
PAST_LEN = 16384
PAGE_SIZE = 128

N_META = 16
N_AB = (DEPTH + 1) // 2
N_CD = DEPTH // 2
MLA_HEADS = 8
MLA_NOPE = 64
MLA_ROPE = 32
MLA_V = 64
Q_LORA = 256
KV_LORA = 256
DIFF_HEADS = 4
DIFF_QK = 64
DIFF_V = 128
RET_HEADS = 4
RET_DK = 64
RET_DV = 128
ML_HEADS = 4
ML_DQK = 64
ML_DV = 128
D_FF = 4 * D_MODEL
CHUNK = 128
Q_BLOCK = 128
ROPE_BASE = 10000.0
DN_ALPHA = (2 * DEPTH) ** 0.25
DN_BETA = (8 * DEPTH) ** -0.25
NEG_INF = -1e30
EPS = 1e-5
AB_WIDTHS = (Q_LORA, KV_LORA, MLA_ROPE, DIFF_HEADS * 2 * DIFF_QK, DIFF_HEADS * 2 * DIFF_QK, DIFF_HEADS * DIFF_V)
CD_WIDTHS = (RET_HEADS * RET_DK, RET_HEADS * RET_DK, RET_HEADS * RET_DV, RET_HEADS * RET_DV,
             ML_HEADS * ML_DQK, ML_HEADS * ML_DQK, ML_HEADS * ML_DV, ML_HEADS * ML_DV, ML_HEADS, ML_HEADS)
AB_IN = sum(AB_WIDTHS)
CD_IN = sum(CD_WIDTHS)
AB_OUT = MLA_HEADS * MLA_V + DIFF_HEADS * DIFF_V
CD_OUT = RET_HEADS * RET_DV + ML_HEADS * ML_DV

kernel_name = 'hybrid_mla_diffattn_retnet_mlstm_step'

F32 = jnp.float32


def _split(h, widths):
    return jnp.split(h, [int(v) for v in np.cumsum(widths)[:-1]], axis=-1)


def _heads(a, n):
    b, t, _ = a.shape
    return a.reshape(b, t, n, -1).transpose(0, 2, 1, 3)


def _merge_heads(a):
    b, h, t, d = a.shape
    return a.transpose(0, 2, 1, 3).reshape(b, t, h * d)


def _layer_norm(x, g, b):
    xf = x.astype(F32)
    mu = xf.mean(-1, keepdims=True)
    var = jnp.square(xf - mu).mean(-1, keepdims=True)
    return ((xf - mu) * lax.rsqrt(var + EPS) * g + b).astype(x.dtype)


def _head_norm(x):
    xf = x.astype(F32)
    mu = xf.mean(-1, keepdims=True)
    var = jnp.square(xf - mu).mean(-1, keepdims=True)
    return (xf - mu) * lax.rsqrt(var + EPS)


def _rms_norm(x, g):
    xf = x.astype(F32)
    return (xf * lax.rsqrt(jnp.square(xf).mean(-1, keepdims=True) + EPS) * g).astype(x.dtype)


def _rope(x, pos):
    half = x.shape[-1] // 2
    freqs = ROPE_BASE ** (-jnp.arange(half, dtype=F32) / half)
    ang = pos.astype(F32)[:, None] * freqs[None, :]
    cos, sin = jnp.cos(ang), jnp.sin(ang)
    xf = x.astype(F32)
    x1, x2 = xf[..., :half], xf[..., half:]
    return jnp.concatenate([x1 * cos - x2 * sin, x1 * sin + x2 * cos], axis=-1).astype(x.dtype)


def _partial(logits, values_fn, ks):
    m = logits.max(-1)
    p = jnp.exp(logits - m[..., None])
    return m, p.sum(-1), values_fn(p, ks)


def _merge(a, b):
    ma, la, xa = a
    mb, lb, xb = b
    m = jnp.maximum(ma, mb)
    sa, sb = jnp.exp(ma - m), jnp.exp(mb - m)
    return m, la * sa + lb * sb, xa * sa[..., None] + xb * sb[..., None]


def _attend_prompt(logits_fn, values_fn, qs, ks):
    total = ks[0].shape[1]
    n_real = total - N_META
    n_blocks = n_real // Q_BLOCK
    key_pos = jnp.arange(total)

    def run(qb, kb, q_pos):
        k_pos = key_pos[:kb[0].shape[1]]
        logits = jnp.where(k_pos[None, :] <= q_pos[:, None], logits_fn(qb, kb), NEG_INF)
        _, l, acc = _partial(logits, values_fn, kb)
        return acc / l[..., None]

    meta_out = run(tuple(q[..., :N_META, :] for q in qs), tuple(k[:, :N_META] for k in ks), jnp.arange(N_META))
    blocks = tuple(jnp.moveaxis(q[..., N_META:, :].reshape(q.shape[:-2] + (n_blocks, Q_BLOCK, q.shape[-1])), -3, 0)
                   for q in qs)
    starts = N_META + Q_BLOCK * jnp.arange(n_blocks)
    out = lax.map(lambda a: run(a[0], ks, a[1] + jnp.arange(Q_BLOCK)), (blocks, starts))
    out = jnp.moveaxis(out, 0, -3)
    out = out.reshape(out.shape[:-3] + (n_real, out.shape[-1]))
    return jnp.concatenate([meta_out, out], axis=-2)


def _attend_sample(logits_fn, values_fn, qs, ks_new, pools, page_table, li):
    t = ks_new[0].shape[1]
    causal = jnp.tril(jnp.ones((t, t), dtype=bool))
    part = _partial(jnp.where(causal, logits_fn(qs, ks_new), NEG_INF), values_fn, ks_new)

    def step(carry, phys):
        kp = tuple(pool[li, phys] for pool in pools)
        return _merge(carry, _partial(logits_fn(qs, kp), values_fn, kp)), None

    (_, l, acc), _ = lax.scan(step, part, page_table.T)
    return acc / l[..., None]


def _to_chunks(a):
    b, h, t = a.shape[:3]
    return jnp.moveaxis(a.reshape((b, h, t // CHUNK, CHUNK) + a.shape[3:]), 2, 0)


def _from_chunks(a):
    a = jnp.moveaxis(a, 0, 2)
    b, h, nc, ch = a.shape[:4]
    return a.reshape((b, h, nc * ch) + a.shape[4:])


def _prompt_chunked(chunk_fn, state0, arrays):
    meta = tuple(a[:, :, :N_META] for a in arrays)
    real = tuple(_to_chunks(a[:, :, N_META:]) for a in arrays)
    y0, st = chunk_fn(state0, *meta)

    def step(carry, xs):
        y, carry = chunk_fn(carry, *xs)
        return carry, y

    st, ys = lax.scan(step, st, real)
    return jnp.concatenate([y0, _from_chunks(ys)], axis=2), st


def _run_single(chunk_fn, state, arrays):
    return chunk_fn(state, *arrays)


def _retention_chunk(log_gamma, s, q, k, v):
    L = q.shape[2]
    idx = jnp.arange(L, dtype=F32)
    diff = idx[:, None] - idx[None, :]
    lg = log_gamma[:, None, None]
    decay = jnp.where(diff >= 0, jnp.exp(lg * jnp.maximum(diff, 0.0)), 0.0)
    inner = jnp.einsum('bhtd,bhsd->bhts', q, k) * decay
    o = (jnp.einsum('bhts,bhse->bhte', inner, v)
         + jnp.einsum('bhtd,bhde->bhte', q * jnp.exp(lg * (idx[:, None] + 1.0)), s))
    s_new = (jnp.exp(lg * L) * s
             + jnp.einsum('bhsd,bhse->bhde', k * jnp.exp(lg * (L - 1.0 - idx)[:, None]), v))
    return o, s_new


def _mlstm_chunk(state, q, k, v, i_pre, log_f):
    c, n, m = state
    L = q.shape[2]
    b = jnp.cumsum(log_f, axis=-1)
    causal = jnp.tril(jnp.ones((L, L), dtype=bool))
    d = jnp.where(causal, b[..., :, None] - b[..., None, :] + i_pre[..., None, :], NEG_INF)
    g = b + m[..., None]
    m_t = jnp.maximum(g, d.max(-1))
    w = jnp.exp(d - m_t[..., None])
    gs = jnp.exp(g - m_t)
    a = jnp.einsum('bhtd,bhsd->bhts', q, k) * w
    num = jnp.einsum('bhts,bhse->bhte', a, v) + gs[..., None] * jnp.einsum('bhed,bhtd->bhte', c, q)
    den = a.sum(-1) + gs * jnp.einsum('bhd,bhtd->bht', n, q)
    h = num / jnp.maximum(jnp.abs(den), jnp.exp(-m_t))[..., None]
    w_last, g_last = w[..., -1, :], gs[..., -1]
    c_new = g_last[..., None, None] * c + jnp.einsum('bhs,bhse,bhsd->bhed', w_last, v, k)
    n_new = g_last[..., None] * n + jnp.einsum('bhs,bhsd->bhd', w_last, k)
    return h, (c_new, n_new, m_t[..., -1])


def _mla_logits(qs, ks):
    q_lat, q_pe = qs
    c, kr = ks
    return (jnp.einsum('bhtc,bkc->bhtk', q_lat, c, preferred_element_type=F32)
            + jnp.einsum('bhtr,bkr->bhtk', q_pe, kr, preferred_element_type=F32))


def _mla_values(p, ks):
    return jnp.einsum('bhtk,bkc->bhtc', p, ks[0])


def _mla(c_q, c_kv, k_r, pos, p, attend):
    b, t, _ = c_q.shape
    scale = (MLA_NOPE + MLA_ROPE) ** -0.5
    q = _heads(_rms_norm(c_q, p['q_norm']) @ p['w_uq'], MLA_HEADS)
    q_lat = jnp.einsum('bhtn,chn->bhtc', q[..., :MLA_NOPE], p['w_uk']) * scale
    q_pe = _rope(q[..., MLA_NOPE:], pos) * scale
    c = _rms_norm(c_kv, p['kv_norm'])
    kr = _rope(k_r, pos)
    o_lat = attend(_mla_logits, _mla_values, (q_lat, q_pe), (c, kr))
    o = jnp.einsum('bhtc,chv->bthv', o_lat, p['w_uv']).reshape(b, t, MLA_HEADS * MLA_V)
    return o, (c, kr)


def _diff_logits(qs, ks):
    return jnp.einsum('bhmtd,bkhmd->bhmtk', qs[0], ks[0], preferred_element_type=F32)


def _diff_values(p, ks):
    return jnp.einsum('bhmtk,bkhv->bhmtv', p, ks[1])


def _diff(dq, dk, dv, p, lambda_init, attend):
    b, t, _ = dq.shape
    q = dq.reshape(b, t, DIFF_HEADS, 2, DIFF_QK).transpose(0, 2, 3, 1, 4) * DIFF_QK ** -0.5
    k = dk.reshape(b, t, DIFF_HEADS, 2, DIFF_QK)
    v = dv.reshape(b, t, DIFF_HEADS, DIFF_V)
    o = attend(_diff_logits, _diff_values, (q,), (k, v))
    lam = (jnp.exp(jnp.sum(p['lq1'].astype(F32) * p['lk1']))
           - jnp.exp(jnp.sum(p['lq2'].astype(F32) * p['lk2'])) + lambda_init)
    o = o[:, :, 0] - lam * o[:, :, 1]
    o = _rms_norm(o, p['subln']) * (1.0 - lambda_init)
    return _merge_heads(o), (k, v)


def _ab_mixer(x, pos, p, lambda_init, attend_mla, attend_diff):
    c_q, c_kv, k_r, dq, dk, dv = _split(x @ p['w_in'], AB_WIDTHS)
    o_mla, (c, kr) = _mla(c_q, c_kv, k_r, pos, p, attend_mla)
    o_diff, (k, v) = _diff(dq, dk, dv, p, lambda_init, attend_diff)
    o = jnp.concatenate([o_mla, o_diff], axis=-1).astype(x.dtype) @ p['w_out']
    return o, (c, kr, k, v)


def _cd_mixer(x, pos, p, ret_state, ml_state, run):
    rq, rk, rv, rg, mq, mk, mv, mo, mi, mf = _split(x @ p['w_in'], CD_WIDTHS)
    q = _rope(_heads(rq, RET_HEADS), pos).astype(F32)
    k = _rope(_heads(rk, RET_HEADS), pos).astype(F32) * RET_DK ** -0.5
    v = _heads(rv, RET_HEADS).astype(F32)
    log_gamma = jnp.log1p(-jnp.power(2.0, -5.0 - jnp.arange(RET_HEADS, dtype=F32)))
    o_ret, ret_new = run(functools.partial(_retention_chunk, log_gamma), ret_state, (q, k, v))
    o_ret = _merge_heads(_head_norm(o_ret)) * jax.nn.silu(rg.astype(F32))
    q = _heads(mq, ML_HEADS).astype(F32)
    k = _heads(mk, ML_HEADS).astype(F32) * ML_DQK ** -0.5
    v = _heads(mv, ML_HEADS).astype(F32)
    i_pre = (mi.astype(F32) + p['b_i']).transpose(0, 2, 1)
    log_f = jax.nn.log_sigmoid(mf.astype(F32) + p['b_f']).transpose(0, 2, 1)
    o_ml, ml_new = run(_mlstm_chunk, ml_state, (q, k, v, i_pre, log_f))
    o_ml = _merge_heads(_head_norm(o_ml)) * p['norm'] * jax.nn.sigmoid(mo.astype(F32))
    o = jnp.concatenate([o_ret, o_ml], axis=-1).astype(x.dtype) @ p['w_out']
    return o, ret_new, ml_new


def _mlp(x, w1, w2):
    h = jax.nn.relu(x @ w1)
    return (h * h) @ w2


def setup_inputs(seed: int = 0) -> dict:
    key = jax.random.key(seed)
    ks = jax.random.split(key, 40)

    def nrm(k, shape, scale=1.0):
        return jax.random.normal(k, shape, F32) * scale

    n_pages = PAST_LEN // PAGE_SIZE
    n_used = DEC_BATCH * n_pages
    n_pool = n_used + n_used // 4
    page_table = jax.random.permutation(ks[10], n_pool)[:n_used].reshape(DEC_BATCH, n_pages).astype(jnp.int32)
    return {
        'x_prompt': nrm(ks[0], (BATCH, SEQ, D_MODEL)),
        'x_sample': nrm(ks[1], (DEC_BATCH, DEC_SEQ, D_MODEL)),
        'cache_mla_latent': nrm(ks[2], (N_AB, n_pool, PAGE_SIZE, KV_LORA)),
        'cache_mla_rope': nrm(ks[3], (N_AB, n_pool, PAGE_SIZE, MLA_ROPE)),
        'cache_diff_k': nrm(ks[4], (N_AB, n_pool, PAGE_SIZE, DIFF_HEADS, 2, DIFF_QK)),
        'cache_diff_v': nrm(ks[5], (N_AB, n_pool, PAGE_SIZE, DIFF_HEADS, DIFF_V)),
        'state_ret': nrm(ks[6], (N_CD, DEC_BATCH, RET_HEADS, RET_DK, RET_DV)),
        'state_mlstm_C': nrm(ks[7], (N_CD, DEC_BATCH, ML_HEADS, ML_DV, ML_DQK)),
        'state_mlstm_n': nrm(ks[8], (N_CD, DEC_BATCH, ML_HEADS, ML_DQK)),
        'state_mlstm_m': nrm(ks[9], (N_CD, DEC_BATCH, ML_HEADS)),
        'page_table': page_table,
        'meta_tokens': nrm(ks[11], (N_META, D_MODEL)),
        'ab_w_in': nrm(ks[12], (N_AB, D_MODEL, AB_IN), D_MODEL ** -0.5),
        'mla_q_norm': 1.0 + nrm(ks[13], (N_AB, Q_LORA), 0.1),
        'mla_w_uq': nrm(ks[14], (N_AB, Q_LORA, MLA_HEADS * (MLA_NOPE + MLA_ROPE)), Q_LORA ** -0.5),
        'mla_kv_norm': 1.0 + nrm(ks[15], (N_AB, KV_LORA), 0.1),
        'mla_w_uk': nrm(ks[16], (N_AB, KV_LORA, MLA_HEADS, MLA_NOPE), KV_LORA ** -0.5),
        'mla_w_uv': nrm(ks[17], (N_AB, KV_LORA, MLA_HEADS, MLA_V), KV_LORA ** -0.5),
        'diff_lambda_q1': nrm(ks[18], (N_AB, DIFF_QK), 0.1),
        'diff_lambda_k1': nrm(ks[19], (N_AB, DIFF_QK), 0.1),
        'diff_lambda_q2': nrm(ks[20], (N_AB, DIFF_QK), 0.1),
        'diff_lambda_k2': nrm(ks[21], (N_AB, DIFF_QK), 0.1),
        'diff_subln': 1.0 + nrm(ks[22], (N_AB, DIFF_V), 0.1),
        'ab_w_out': nrm(ks[23], (N_AB, AB_OUT, D_MODEL), DN_BETA * AB_OUT ** -0.5),
        'cd_w_in': nrm(ks[24], (N_CD, D_MODEL, CD_IN), D_MODEL ** -0.5),
        'ml_b_i': nrm(ks[25], (N_CD, ML_HEADS), 0.1),
        'ml_b_f': jnp.linspace(3.0, 6.0, ML_HEADS, dtype=F32)[None, :] + nrm(ks[26], (N_CD, ML_HEADS), 0.1),
        'ml_norm': 1.0 + nrm(ks[27], (N_CD, ML_HEADS * ML_DV), 0.1),
        'cd_w_out': nrm(ks[28], (N_CD, CD_OUT, D_MODEL), DN_BETA * CD_OUT ** -0.5),
        'ln1_g': 1.0 + nrm(ks[29], (DEPTH, D_MODEL), 0.1),
        'ln1_b': nrm(ks[30], (DEPTH, D_MODEL), 0.02),
        'ln2_g': 1.0 + nrm(ks[31], (DEPTH, D_MODEL), 0.1),
        'ln2_b': nrm(ks[32], (DEPTH, D_MODEL), 0.02),
        'mlp_w1': nrm(ks[33], (DEPTH, D_MODEL, D_FF), D_MODEL ** -0.5),
        'mlp_w2': nrm(ks[34], (DEPTH, D_FF, D_MODEL), DN_BETA * D_FF ** -0.5),
    }


def reference(x_prompt, x_sample, cache_mla_latent, cache_mla_rope, cache_diff_k, cache_diff_v,
              state_ret, state_mlstm_C, state_mlstm_n, state_mlstm_m, page_table,
              meta_tokens, ab_w_in, mla_q_norm, mla_w_uq, mla_kv_norm, mla_w_uk, mla_w_uv,
              diff_lambda_q1, diff_lambda_k1, diff_lambda_q2, diff_lambda_k2, diff_subln, ab_w_out,
              cd_w_in, ml_b_i, ml_b_f, ml_norm, cd_w_out,
              ln1_g, ln1_b, ln2_g, ln2_b, mlp_w1, mlp_w2):
    b = x_prompt.shape[0]
    xp = jnp.concatenate([jnp.broadcast_to(meta_tokens.astype(x_prompt.dtype)[None], (b, N_META, D_MODEL)), x_prompt],
                         axis=1)
    xs = x_sample
    pos_p = jnp.arange(xp.shape[1])
    pos_s = PAST_LEN + jnp.arange(xs.shape[1])
    pr_c, pr_kr, pr_k, pr_v, pr_s, pr_C, pr_n, pr_m = [], [], [], [], [], [], [], []
    sm_c, sm_kr, sm_k, sm_v, sm_s, sm_C, sm_n, sm_m = [], [], [], [], [], [], [], []
    for layer in range(DEPTH):
        if layer % 2 == 0:
            li = layer // 2
            p = {'w_in': ab_w_in[li], 'q_norm': mla_q_norm[li], 'w_uq': mla_w_uq[li], 'kv_norm': mla_kv_norm[li],
                 'w_uk': mla_w_uk[li], 'w_uv': mla_w_uv[li], 'lq1': diff_lambda_q1[li], 'lk1': diff_lambda_k1[li],
                 'lq2': diff_lambda_q2[li], 'lk2': diff_lambda_k2[li], 'subln': diff_subln[li], 'w_out': ab_w_out[li]}
            lambda_init = 0.8 - 0.6 * math.exp(-0.3 * layer)
            mix_p, rows_p = _ab_mixer(xp, pos_p, p, lambda_init, _attend_prompt, _attend_prompt)
            att_mla = functools.partial(_attend_sample, pools=(cache_mla_latent, cache_mla_rope),
                                        page_table=page_table, li=li)
            att_diff = functools.partial(_attend_sample, pools=(cache_diff_k, cache_diff_v),
                                         page_table=page_table, li=li)
            mix_s, rows_s = _ab_mixer(xs, pos_s, p, lambda_init, att_mla, att_diff)
            for lst, r in zip((pr_c, pr_kr, pr_k, pr_v), rows_p):
                lst.append(r)
            for lst, r in zip((sm_c, sm_kr, sm_k, sm_v), rows_s):
                lst.append(r)
        else:
            lj = layer // 2
            p = {'w_in': cd_w_in[lj], 'b_i': ml_b_i[lj], 'b_f': ml_b_f[lj], 'norm': ml_norm[lj], 'w_out': cd_w_out[lj]}
            s0 = jnp.zeros((b, RET_HEADS, RET_DK, RET_DV), F32)
            ml0 = (jnp.zeros((b, ML_HEADS, ML_DV, ML_DQK), F32), jnp.zeros((b, ML_HEADS, ML_DQK), F32),
                   jnp.zeros((b, ML_HEADS), F32))
            mix_p, ret_p, ml_p = _cd_mixer(xp, pos_p, p, s0, ml0, _prompt_chunked)
            mix_s, ret_s, ml_s = _cd_mixer(xs, pos_s, p, state_ret[lj],
                                           (state_mlstm_C[lj], state_mlstm_n[lj], state_mlstm_m[lj]), _run_single)
            pr_s.append(ret_p)
            sm_s.append(ret_s)
            for lst, r in zip((pr_C, pr_n, pr_m), ml_p):
                lst.append(r)
            for lst, r in zip((sm_C, sm_n, sm_m), ml_s):
                lst.append(r)
        xp = _layer_norm(DN_ALPHA * xp + mix_p, ln1_g[layer], ln1_b[layer])
        xs = _layer_norm(DN_ALPHA * xs + mix_s, ln1_g[layer], ln1_b[layer])
        w1, w2 = mlp_w1[layer], mlp_w2[layer]
        xp = _layer_norm(DN_ALPHA * xp + _mlp(xp, w1, w2), ln2_g[layer], ln2_b[layer])
        xs = _layer_norm(DN_ALPHA * xs + _mlp(xs, w1, w2), ln2_g[layer], ln2_b[layer])
    y_prompt = xp[:, N_META:]
    y_sample = xs
    return (y_prompt, y_sample,
            jnp.stack(pr_c), jnp.stack(pr_kr), jnp.stack(pr_k), jnp.stack(pr_v),
            jnp.stack(pr_s), jnp.stack(pr_C), jnp.stack(pr_n), jnp.stack(pr_m),
            jnp.stack(sm_c), jnp.stack(sm_kr), jnp.stack(sm_k), jnp.stack(sm_v),
            jnp.stack(sm_s), jnp.stack(sm_C), jnp.stack(sm_n), jnp.stack(sm_m))
```

```python
import functools
import math

import jax
import jax.numpy as jnp
from jax import lax
from jax.experimental import pallas as pl
from jax.experimental.pallas import tpu as pltpu

F32 = jnp.float32
BF16 = jnp.bfloat16

ROPE_BASE = 10000.0
NEG_INF = -1e30
EPS = 1e-5

LANES = 128
MB = 128
VMEM_CAP_BYTES = 56 << 20


def _vmem(nbytes):
    return int(min(VMEM_CAP_BYTES, max(16 << 20, nbytes)))


def _params(sem, nbytes):
    return pltpu.CompilerParams(dimension_semantics=sem, vmem_limit_bytes=_vmem(nbytes))


def _const_spec(arr):
    nd = arr.ndim
    return pl.BlockSpec(arr.shape, lambda *_: (0,) * nd, pipeline_mode=pl.Buffered(1))


def _tile(n, pref):
    if n <= pref:
        return n
    best = MB
    for t in range(MB, pref + 1, MB):
        if n % t == 0:
            best = t
    assert n % best == 0, (n, pref)
    return best


def _nt(a, b):
    return lax.dot_general(a, b, (((1,), (1,)), ((), ())), preferred_element_type=F32)


def _nn(a, b):
    return jnp.dot(a, b, preferred_element_type=F32)


def _tn(a, b):
    return _nn(a.T, b)


def _rms(x):
    return x * lax.rsqrt(jnp.mean(x * x, axis=-1, keepdims=True) + EPS)


def _layer_norm(x, g, b):
    mu = jnp.mean(x, axis=-1, keepdims=True)
    xc = x - mu
    var = jnp.mean(xc * xc, axis=-1, keepdims=True)
    return xc * lax.rsqrt(var + EPS) * g + b


def _head_norm(x):
    mu = jnp.mean(x, axis=-1, keepdims=True)
    xc = x - mu
    var = jnp.mean(xc * xc, axis=-1, keepdims=True)
    return xc * lax.rsqrt(var + EPS)


def _wprod_kernel(a_ref, b_ref, o_ref, *, trans_b):
    a = a_ref[...].astype(BF16)
    b = b_ref[...].astype(BF16)
    o_ref[...] = _nt(a, b) if trans_b else _nn(a, b)


def _wprod(a, b, trans_b):
    h, m, _ = a.shape
    n = b.shape[1] if trans_b else b.shape[2]
    return pl.pallas_call(
        functools.partial(_wprod_kernel, trans_b=trans_b),
        out_shape=jax.ShapeDtypeStruct((h, m, n), F32),
        grid=(h,),
        in_specs=[pl.BlockSpec((None,) + a.shape[1:], lambda i: (i, 0, 0)),
                  pl.BlockSpec((None,) + b.shape[1:], lambda i: (i, 0, 0))],
        out_specs=pl.BlockSpec((None, m, n), lambda i: (i, 0, 0)),
        compiler_params=_params(("arbitrary",), 16 << 20),
        name="weight_product",
    )(a, b)


def _ab_in_kernel(x_ref, cos_ref, sin_ref, w_ref, wq_ref, qn_ref, kvn_ref,
                  c_ref, kr_ref, kcat_ref, q_ref, dq_ref, dk_ref, dv_ref, dkb_ref, dvb_ref,
                  *, ql, kl, dw, rope, n_mla, n_diff, dq_scale):
    xb = x_ref[...].astype(BF16)
    cos = cos_ref[...]
    sin = sin_ref[...]

    def mm(lo, width):
        return _nn(xb, w_ref[:, lo:lo + width])

    o_ckv, o_dq, o_dk, o_dv, o_kr = ql, ql + kl, ql + kl + dw, ql + kl + 2 * dw, ql + kl + 3 * dw
    c = _rms(mm(o_ckv, kl)) * kvn_ref[...]
    krr = mm(o_kr, 2 * LANES)
    kr = krr[:, :LANES] * cos + krr[:, LANES:] * sin
    c_ref[...] = c
    kr_ref[...] = kr[:, :rope]
    kcat_ref[:, :kl] = c.astype(BF16)
    kcat_ref[:, kl:] = kr.astype(BF16)
    dq = mm(o_dq, dw) * dq_scale
    dk = mm(o_dk, dw)
    dv = mm(o_dv, dw)
    dk_ref[...] = dk
    dv_ref[...] = dv
    for h in range(n_diff):
        sl = slice(h * LANES, (h + 1) * LANES)
        dq_ref[h] = dq[:, sl].astype(BF16)
        dkb_ref[h] = dk[:, sl].astype(BF16)
        dvb_ref[h] = dv[:, sl].astype(BF16)
    cq = (_rms(mm(0, ql)) * qn_ref[...]).astype(BF16)
    hw = kl + 2 * LANES
    for h in range(n_mla):
        qh = _nn(cq, wq_ref[:, h * hw:(h + 1) * hw])
        q_ref[h, :, :kl] = qh[:, :kl].astype(BF16)
        q_ref[h, :, kl:] = (qh[:, kl:kl + LANES] * cos + qh[:, kl + LANES:] * sin).astype(BF16)


def _ab_in(x, cos, sin, w_aug, wq_big, q_norm, kv_norm, *, tm, dims):
    bx, tx, d = x.shape
    ql, kl, dw, rope, n_mla, n_diff = dims["ql"], dims["kl"], dims["dw"], dims["rope"], dims["n_mla"], dims["n_diff"]
    row = lambda b, i: (b, i, 0)
    hrow = lambda b, i: (b, 0, i, 0)
    tab = lambda b, i: (i, 0)
    out_shape = (
        jax.ShapeDtypeStruct((bx, tx, kl), F32),
        jax.ShapeDtypeStruct((bx, tx, rope), F32),
        jax.ShapeDtypeStruct((bx, tx, kl + LANES), BF16),
        jax.ShapeDtypeStruct((bx, n_mla, tx, kl + LANES), BF16),
        jax.ShapeDtypeStruct((bx, n_diff, tx, LANES), BF16),
        jax.ShapeDtypeStruct((bx, tx, dw), F32),
        jax.ShapeDtypeStruct((bx, tx, dw), F32),
        jax.ShapeDtypeStruct((bx, n_diff, tx, LANES), BF16),
        jax.ShapeDtypeStruct((bx, n_diff, tx, LANES), BF16),
    )
    out_specs = (
        pl.BlockSpec((None, tm, kl), row),
        pl.BlockSpec((None, tm, rope), row),
        pl.BlockSpec((None, tm, kl + LANES), row),
        pl.BlockSpec((None, n_mla, tm, kl + LANES), hrow),
        pl.BlockSpec((None, n_diff, tm, LANES), hrow),
        pl.BlockSpec((None, tm, dw), row),
        pl.BlockSpec((None, tm, dw), row),
        pl.BlockSpec((None, n_diff, tm, LANES), hrow),
        pl.BlockSpec((None, n_diff, tm, LANES), hrow),
    )
    nbytes = (2 * tm * d * 4 + w_aug.size * 2 + wq_big.size * 2
              + 2 * tm * (kl * 4 + LANES * 4 + (kl + LANES) * 2 + n_mla * (kl + LANES) * 2
                          + 3 * n_diff * LANES * 2 + 2 * dw * 4)
              + 6 * tm * dw * 4 + (8 << 20))
    return pl.pallas_call(
        functools.partial(_ab_in_kernel, ql=ql, kl=kl, dw=dw, rope=rope, n_mla=n_mla, n_diff=n_diff,
                          dq_scale=dims["dq_scale"]),
        out_shape=out_shape,
        grid=(bx, tx // tm),
        in_specs=[pl.BlockSpec((None, tm, d), row),
                  pl.BlockSpec((tm, LANES), tab),
                  pl.BlockSpec((tm, LANES), tab),
                  _const_spec(w_aug), _const_spec(wq_big), _const_spec(q_norm), _const_spec(kv_norm)],
        out_specs=out_specs,
        compiler_params=_params(("parallel", "parallel"), nbytes),
        name="ab_in_proj",
    )(x, cos, sin, w_aug, wq_big, q_norm, kv_norm)


def _tri_schedule(nq):
    qi = [i for i in range(nq) for _ in range(i + 1)]
    kj = [j for i in range(nq) for j in range(i + 1)]
    return jnp.asarray(qi, jnp.int32), jnp.asarray(kj, jnp.int32)


def _mla_prompt_kernel(qi_ref, kj_ref, q_ref, k_ref, km_ref, o_ref, m_sc, l_sc, acc_sc,
                       *, n_heads, kl, n_meta, tq):
    step = pl.program_id(1)
    qi = qi_ref[step]
    kj = kj_ref[step]

    @pl.when(kj == 0)
    def _init():
        km = km_ref[...]
        vm = km_ref[:, :kl]
        valid = lax.broadcasted_iota(jnp.int32, (tq, MB), 1) < n_meta

        def body(h, carry):
            s = jnp.where(valid, _nt(q_ref[h], km), NEG_INF)
            m = jnp.max(s, axis=1, keepdims=True)
            p = jnp.exp(s - m)
            m_sc[h] = m
            l_sc[h] = jnp.sum(p, axis=1, keepdims=True)
            acc_sc[h] = _nn(p.astype(BF16), vm)
            return carry

        lax.fori_loop(0, n_heads, body, 0)

    k = k_ref[...]
    v = k_ref[:, :kl]
    row = lax.broadcasted_iota(jnp.int32, (tq, tq), 0)
    col = lax.broadcasted_iota(jnp.int32, (tq, tq), 1)
    visible = (col - row) <= (qi - kj) * tq

    def body(h, carry):
        s = jnp.where(visible, _nt(q_ref[h], k), NEG_INF)
        m_prev = m_sc[h]
        m_new = jnp.maximum(m_prev, jnp.max(s, axis=1, keepdims=True))
        alpha = jnp.exp(m_prev - m_new)
        p = jnp.exp(s - m_new)
        l_sc[h] = alpha * l_sc[h] + jnp.sum(p, axis=1, keepdims=True)
        acc_sc[h] = alpha * acc_sc[h] + _nn(p.astype(BF16), v)
        m_sc[h] = m_new
        return carry

    lax.fori_loop(0, n_heads, body, 0)

    @pl.when(kj == qi)
    def _finish():
        for h in range(n_heads):
            o_ref[:, h * kl:(h + 1) * kl] = (acc_sc[h] / l_sc[h]).astype(o_ref.dtype)


def _mla_meta_kernel(q_ref, km_ref, prev_ref, o_ref, *, n_heads, kl, n_meta):
    del prev_ref
    km = km_ref[...]
    vm = km_ref[:, :kl]
    row = lax.broadcasted_iota(jnp.int32, (MB, MB), 0)
    col = lax.broadcasted_iota(jnp.int32, (MB, MB), 1)
    visible = (col <= row) & (col < n_meta)
    for h in range(n_heads):
        s = jnp.where(visible, _nt(q_ref[h], km), NEG_INF)
        p = jnp.exp(s - jnp.max(s, axis=1, keepdims=True))
        o = _nn(p.astype(BF16), vm) / jnp.sum(p, axis=1, keepdims=True)
        o_ref[:, h * kl:(h + 1) * kl] = o.astype(o_ref.dtype)


def _mla_prompt(q, kcat, *, t_real, n_meta, tq):
    b, n_heads, tp, kw = q.shape
    kl = kw - LANES
    nq = t_real // tq
    qi, kj = _tri_schedule(nq)
    meta_blk = t_real // MB
    grid_spec = pltpu.PrefetchScalarGridSpec(
        num_scalar_prefetch=2,
        grid=(b, int(qi.shape[0])),
        in_specs=[pl.BlockSpec((None, n_heads, tq, kw), lambda bb, s, qi_r, kj_r: (bb, 0, qi_r[s], 0)),
                  pl.BlockSpec((None, tq, kw), lambda bb, s, qi_r, kj_r: (bb, kj_r[s], 0)),
                  pl.BlockSpec((None, MB, kw), lambda bb, s, qi_r, kj_r: (bb, meta_blk, 0))],
        out_specs=pl.BlockSpec((None, tq, n_heads * kl), lambda bb, s, qi_r, kj_r: (bb, qi_r[s], 0)),
        scratch_shapes=[pltpu.VMEM((n_heads, tq, 1), F32), pltpu.VMEM((n_heads, tq, 1), F32),
                        pltpu.VMEM((n_heads, tq, kl), F32)],
    )
    nbytes = (2 * n_heads * tq * kw * 2 + 2 * tq * kw * 2 + 2 * tq * n_heads * kl * 2
              + n_heads * tq * (kl + 2 * LANES) * 4 + 8 * tq * tq * 4 + (8 << 20))
    o_real = pl.pallas_call(
        functools.partial(_mla_prompt_kernel, n_heads=n_heads, kl=kl, n_meta=n_meta, tq=tq),
        out_shape=jax.ShapeDtypeStruct((b, tp, n_heads * kl), BF16),
        grid_spec=grid_spec,
        compiler_params=_params(("parallel", "arbitrary"), nbytes),
        name="mla_prompt_attention",
    )(qi, kj, q, kcat, kcat)
    return pl.pallas_call(
        functools.partial(_mla_meta_kernel, n_heads=n_heads, kl=kl, n_meta=n_meta),
        out_shape=jax.ShapeDtypeStruct((b, tp, n_heads * kl), BF16),
        grid=(b,),
        in_specs=[pl.BlockSpec((None, n_heads, MB, kw), lambda bb: (bb, 0, meta_blk, 0)),
                  pl.BlockSpec((None, MB, kw), lambda bb: (bb, meta_blk, 0)),
                  pl.BlockSpec(memory_space=pl.ANY)],
        out_specs=pl.BlockSpec((None, MB, n_heads * kl), lambda bb: (bb, meta_blk, 0)),
        input_output_aliases={2: 0},
        compiler_params=_params(("parallel",), 16 << 20),
        name="mla_meta_attention",
    )(q, kcat, o_real)


def _lambda_diff(lam_ref, lambda_init):
    lam = lam_ref[...]
    return (jnp.exp(jnp.sum(lam[0:1] * lam[1:2], axis=1, keepdims=True))
            - jnp.exp(jnp.sum(lam[2:3] * lam[3:4], axis=1, keepdims=True)) + lambda_init)


def _diff_split(q, qk):
    lane = lax.broadcasted_iota(jnp.int32, q.shape, 1)
    zero = jnp.zeros_like(q)
    return jnp.concatenate([jnp.where(lane < qk, q, zero), jnp.where(lane >= qk, q, zero)], axis=0)


def _diff_prompt_kernel(qi_ref, kj_ref, q_ref, k_ref, v_ref, km_ref, vm_ref, lam_ref, sub_ref,
                        o_ref, m_sc, l_sc, acc_sc, *, n_heads, qk, n_meta, tq, lambda_init):
    step = pl.program_id(1)
    qi = qi_ref[step]
    kj = kj_ref[step]

    @pl.when(kj == 0)
    def _init():
        valid = lax.broadcasted_iota(jnp.int32, (2 * tq, MB), 1) < n_meta

        def body(h, carry):
            s = jnp.where(valid, _nt(_diff_split(q_ref[h], qk), km_ref[h]), NEG_INF)
            m = jnp.max(s, axis=1, keepdims=True)
            p = jnp.exp(s - m)
            m_sc[h] = m
            l_sc[h] = jnp.sum(p, axis=1, keepdims=True)
            acc_sc[h] = _nn(p.astype(BF16), vm_ref[h])
            return carry

        lax.fori_loop(0, n_heads, body, 0)

    row = lax.broadcasted_iota(jnp.int32, (2 * tq, tq), 0)
    row = jnp.where(row >= tq, row - tq, row)
    col = lax.broadcasted_iota(jnp.int32, (2 * tq, tq), 1)
    visible = (col - row) <= (qi - kj) * tq

    def body(h, carry):
        s = jnp.where(visible, _nt(_diff_split(q_ref[h], qk), k_ref[h]), NEG_INF)
        m_prev = m_sc[h]
        m_new = jnp.maximum(m_prev, jnp.max(s, axis=1, keepdims=True))
        alpha = jnp.exp(m_prev - m_new)
        p = jnp.exp(s - m_new)
        l_sc[h] = alpha * l_sc[h] + jnp.sum(p, axis=1, keepdims=True)
        acc_sc[h] = alpha * acc_sc[h] + _nn(p.astype(BF16), v_ref[h])
        m_sc[h] = m_new
        return carry

    lax.fori_loop(0, n_heads, body, 0)

    @pl.when(kj == qi)
    def _finish():
        lam = _lambda_diff(lam_ref, lambda_init)
        for h in range(n_heads):
            o = acc_sc[h] / l_sc[h]
            o = o[:tq] - lam * o[tq:]
            o = _rms(o) * sub_ref[...] * (1.0 - lambda_init)
            o_ref[:, h * LANES:(h + 1) * LANES] = o.astype(o_ref.dtype)


def _diff_meta_kernel(q_ref, km_ref, vm_ref, lam_ref, sub_ref, prev_ref, o_ref,
                      *, n_heads, qk, n_meta, lambda_init):
    del prev_ref
    row = lax.broadcasted_iota(jnp.int32, (2 * MB, MB), 0)
    row = jnp.where(row >= MB, row - MB, row)
    col = lax.broadcasted_iota(jnp.int32, (2 * MB, MB), 1)
    visible = (col <= row) & (col < n_meta)
    lam = _lambda_diff(lam_ref, lambda_init)
    for h in range(n_heads):
        s = jnp.where(visible, _nt(_diff_split(q_ref[h], qk), km_ref[h]), NEG_INF)
        p = jnp.exp(s - jnp.max(s, axis=1, keepdims=True))
        o = _nn(p.astype(BF16), vm_ref[h]) / jnp.sum(p, axis=1, keepdims=True)
        o = o[:MB] - lam * o[MB:]
        o = _rms(o) * sub_ref[...] * (1.0 - lambda_init)
        o_ref[:, h * LANES:(h + 1) * LANES] = o.astype(o_ref.dtype)


def _diff_prompt(q, k, v, lam, subln, *, t_real, n_meta, tq, qk, lambda_init):
    b, n_heads, tp, _ = q.shape
    nq = t_real // tq
    qi, kj = _tri_schedule(nq)
    meta_blk = t_real // MB
    qmap = lambda bb, s, qi_r, kj_r: (bb, 0, qi_r[s], 0)
    kmap = lambda bb, s, qi_r, kj_r: (bb, 0, kj_r[s], 0)
    mmap = lambda bb, s, qi_r, kj_r: (bb, 0, meta_blk, 0)
    cmap = lambda bb, s, qi_r, kj_r: (0, 0)
    grid_spec = pltpu.PrefetchScalarGridSpec(
        num_scalar_prefetch=2,
        grid=(b, int(qi.shape[0])),
        in_specs=[pl.BlockSpec((None, n_heads, tq, LANES), qmap),
                  pl.BlockSpec((None, n_heads, tq, LANES), kmap),
                  pl.BlockSpec((None, n_heads, tq, LANES), kmap),
                  pl.BlockSpec((None, n_heads, MB, LANES), mmap),
                  pl.BlockSpec((None, n_heads, MB, LANES), mmap),
                  pl.BlockSpec(lam.shape, cmap),
                  pl.BlockSpec(subln.shape, cmap)],
        out_specs=pl.BlockSpec((None, tq, n_heads * LANES), lambda bb, s, qi_r, kj_r: (bb, qi_r[s], 0)),
        scratch_shapes=[pltpu.VMEM((n_heads, 2 * tq, 1), F32), pltpu.VMEM((n_heads, 2 * tq, 1), F32),
                        pltpu.VMEM((n_heads, 2 * tq, LANES), F32)],
    )
    nbytes = (6 * n_heads * tq * LANES * 2 + 2 * tq * n_heads * LANES * 2
              + n_heads * 2 * tq * 3 * LANES * 4 + 8 * 2 * tq * tq * 4 + (8 << 20))
    o_real = pl.pallas_call(
        functools.partial(_diff_prompt_kernel, n_heads=n_heads, qk=qk, n_meta=n_meta, tq=tq,
                          lambda_init=lambda_init),
        out_shape=jax.ShapeDtypeStruct((b, tp, n_heads * LANES), BF16),
        grid_spec=grid_spec,
        compiler_params=_params(("parallel", "arbitrary"), nbytes),
        name="diff_prompt_attention",
    )(qi, kj, q, k, v, k, v, lam, subln)
    mmap1 = lambda bb: (bb, 0, meta_blk, 0)
    return pl.pallas_call(
        functools.partial(_diff_meta_kernel, n_heads=n_heads, qk=qk, n_meta=n_meta, lambda_init=lambda_init),
        out_shape=jax.ShapeDtypeStruct((b, tp, n_heads * LANES), BF16),
        grid=(b,),
        in_specs=[pl.BlockSpec((None, n_heads, MB, LANES), mmap1),
                  pl.BlockSpec((None, n_heads, MB, LANES), mmap1),
                  pl.BlockSpec((None, n_heads, MB, LANES), mmap1),
                  pl.BlockSpec(lam.shape, lambda bb: (0, 0)),
                  pl.BlockSpec(subln.shape, lambda bb: (0, 0)),
                  pl.BlockSpec(memory_space=pl.ANY)],
        out_specs=pl.BlockSpec((None, MB, n_heads * LANES), lambda bb: (bb, meta_blk, 0)),
        input_output_aliases={5: 0},
        compiler_params=_params(("parallel",), 16 << 20),
        name="diff_meta_attention",
    )(q, k, v, lam, subln, o_real)


def _softmax_step(s, m_sc, l_sc):
    m_prev = m_sc[...]
    m_new = jnp.maximum(m_prev, jnp.max(s, axis=1, keepdims=True))
    alpha = jnp.exp(m_prev - m_new)
    p = jnp.exp(s - m_new)
    l_sc[...] = alpha * l_sc[...] + jnp.sum(p, axis=1, keepdims=True)
    m_sc[...] = m_new
    return alpha, p


def _mla_decode_kernel(pt_ref, q_ref, cn_ref, krn_ref, *rest, n_pages, kl, rope, t_new):
    del pt_ref
    lat_refs = rest[:n_pages]
    rope_refs = rest[n_pages:2 * n_pages]
    o_ref, m_sc, l_sc, acc_sc = rest[2 * n_pages:]
    j = pl.program_id(1)
    q_lat = q_ref[:, :kl]
    q_pe = q_ref[:, kl:kl + rope]
    rows = q_ref.shape[0]
    page = cn_ref.shape[0]

    @pl.when(j == 0)
    def _init():
        cn = cn_ref[...].astype(BF16)
        s = _nt(q_lat, cn) + _nt(q_pe, krn_ref[...].astype(BF16))
        t = lax.broadcasted_iota(jnp.int32, (rows, page), 0) % t_new
        col = lax.broadcasted_iota(jnp.int32, (rows, page), 1)
        s = jnp.where(col <= t, s, NEG_INF)
        m = jnp.max(s, axis=1, keepdims=True)
        p = jnp.exp(s - m)
        m_sc[...] = m
        l_sc[...] = jnp.sum(p, axis=1, keepdims=True)
        acc_sc[...] = _nn(p.astype(BF16), cn)

    lat = [r[...].astype(BF16) for r in lat_refs]
    s = jnp.concatenate([_nt(q_lat, lat[g]) + _nt(q_pe, rope_refs[g][...].astype(BF16))
                         for g in range(n_pages)], axis=1)
    alpha, p = _softmax_step(s, m_sc, l_sc)
    pv = _nn(p[:, :page].astype(BF16), lat[0])
    for g in range(1, n_pages):
        pv = pv + _nn(p[:, g * page:(g + 1) * page].astype(BF16), lat[g])
    acc_sc[...] = alpha * acc_sc[...] + pv

    @pl.when(j == pl.num_programs(1) - 1)
    def _finish():
        o_ref[...] = (acc_sc[...] / l_sc[...]).astype(o_ref.dtype)


def _page_specs(pool, li, n_pages, n_tab):
    blk = (None, None) + pool.shape[2:]
    specs = []
    for g in range(n_pages):
        specs.append(pl.BlockSpec(
            blk, lambda b, j, pt, g=g: (li, pt[b * n_tab + j * n_pages + g]) + (0,) * (pool.ndim - 2)))
    return specs


def _mla_decode(q, c_new, kr_new, lat_pool, rope_pool, page_table, li, *, t_new, pages_per_step):
    n_seq, rows, kw = q.shape
    kl = kw - LANES
    page, rope = rope_pool.shape[2], rope_pool.shape[3]
    n_tab = page_table.shape[1]
    g = pages_per_step
    seq3 = lambda b, j, pt: (b, 0, 0)
    grid_spec = pltpu.PrefetchScalarGridSpec(
        num_scalar_prefetch=1,
        grid=(n_seq, n_tab // g),
        in_specs=[pl.BlockSpec((None, rows, kw), seq3),
                  pl.BlockSpec((None, page, kl), seq3),
                  pl.BlockSpec((None, page, rope), seq3)]
                 + _page_specs(lat_pool, li, g, n_tab) + _page_specs(rope_pool, li, g, n_tab),
        out_specs=pl.BlockSpec((None, rows, kl), seq3),
        scratch_shapes=[pltpu.VMEM((rows, 1), F32), pltpu.VMEM((rows, 1), F32), pltpu.VMEM((rows, kl), F32)],
    )
    nbytes = 2 * g * page * (kl + LANES) * 4 + 3 * g * page * kl * 2 + (8 << 20)
    return pl.pallas_call(
        functools.partial(_mla_decode_kernel, n_pages=g, kl=kl, rope=rope, t_new=t_new),
        out_shape=jax.ShapeDtypeStruct((n_seq, rows, kl), BF16),
        grid_spec=grid_spec,
        compiler_params=_params(("parallel", "arbitrary"), nbytes),
        name="mla_paged_attention",
    )(page_table.reshape(-1), q, c_new, kr_new, *([lat_pool] * g), *([rope_pool] * g))


def _diff_decode_kernel(pt_ref, q_ref, kn_ref, vn_ref, lam_ref, sub_ref, *rest,
                        n_pages, n_heads, t_new, lambda_init):
    del pt_ref
    k_refs = rest[:n_pages]
    v_refs = rest[n_pages:2 * n_pages]
    o_ref, m_sc, l_sc, acc_sc = rest[2 * n_pages:]
    j = pl.program_id(1)
    q = q_ref[...]
    rows = q.shape[0]
    page = kn_ref.shape[0]

    @pl.when(j == 0)
    def _init():
        s = _nt(q, kn_ref[...].astype(BF16))
        t = lax.broadcasted_iota(jnp.int32, (rows, page), 0) % t_new
        col = lax.broadcasted_iota(jnp.int32, (rows, page), 1)
        s = jnp.where(col <= t, s, NEG_INF)
        m = jnp.max(s, axis=1, keepdims=True)
        p = jnp.exp(s - m)
        m_sc[...] = m
        l_sc[...] = jnp.sum(p, axis=1, keepdims=True)
        acc_sc[...] = _nn(p.astype(BF16), vn_ref[...].astype(BF16))

    s = jnp.concatenate([_nt(q, k_refs[g][...].astype(BF16)) for g in range(n_pages)], axis=1)
    alpha, p = _softmax_step(s, m_sc, l_sc)
    pv = _nn(p[:, :page].astype(BF16), v_refs[0][...].astype(BF16))
    for g in range(1, n_pages):
        pv = pv + _nn(p[:, g * page:(g + 1) * page].astype(BF16), v_refs[g][...].astype(BF16))
    acc_sc[...] = alpha * acc_sc[...] + pv

    @pl.when(j == pl.num_programs(1) - 1)
    def _finish():
        lam = _lambda_diff(lam_ref, lambda_init)
        o = acc_sc[...] / l_sc[...]
        per_head = rows // n_heads
        for h in range(n_heads):
            blk = o[h * per_head:(h + 1) * per_head, h * LANES:(h + 1) * LANES]
            d = blk - lam * pltpu.roll(blk, per_head // 2, 0)
            d = _rms(d) * sub_ref[...] * (1.0 - lambda_init)
            o_ref[h * per_head:(h + 1) * per_head, :] = d.astype(o_ref.dtype)


def _diff_decode(q, k_new, v_new, lam, subln, k_pool, v_pool, page_table, li,
                 *, t_new, n_heads, pages_per_step, lambda_init):
    n_seq, rows, width = q.shape
    page = k_pool.shape[2]
    n_tab = page_table.shape[1]
    g = pages_per_step
    seq3 = lambda b, j, pt: (b, 0, 0)
    cst = lambda b, j, pt: (0, 0)
    grid_spec = pltpu.PrefetchScalarGridSpec(
        num_scalar_prefetch=1,
        grid=(n_seq, n_tab // g),
        in_specs=[pl.BlockSpec((None, rows, width), seq3),
                  pl.BlockSpec((None, page, width), seq3),
                  pl.BlockSpec((None, page, width), seq3),
                  pl.BlockSpec(lam.shape, cst),
                  pl.BlockSpec(subln.shape, cst)]
                 + _page_specs(k_pool, li, g, n_tab) + _page_specs(v_pool, li, g, n_tab),
        out_specs=pl.BlockSpec((None, rows, LANES), seq3),
        scratch_shapes=[pltpu.VMEM((rows, 1), F32), pltpu.VMEM((rows, 1), F32), pltpu.VMEM((rows, width), F32)],
    )
    nbytes = 2 * 2 * g * page * width * 4 + 3 * g * page * width * 2 + (8 << 20)
    return pl.pallas_call(
        functools.partial(_diff_decode_kernel, n_pages=g, n_heads=n_heads, t_new=t_new, lambda_init=lambda_init),
        out_shape=jax.ShapeDtypeStruct((n_seq, rows, LANES), F32),
        grid_spec=grid_spec,
        compiler_params=_params(("parallel", "arbitrary"), nbytes),
        name="diff_paged_attention",
    )(page_table.reshape(-1), q, k_new, v_new, lam, subln, *([k_pool] * g), *([v_pool] * g))


def _post_kernel(*refs, n_act, alpha, tf):
    x_ref = refs[0]
    act_refs = refs[1:1 + n_act]
    w_refs = refs[1 + n_act:1 + 2 * n_act]
    g1_ref, b1_ref, g2_ref, b2_ref, w1_ref, w2_ref, o_ref = refs[1 + 2 * n_act:]
    mix = _nn(act_refs[0][...], w_refs[0][...])
    for a_ref, w_ref in zip(act_refs[1:], w_refs[1:]):
        mix = mix + _nn(a_ref[...], w_ref[...])
    x1 = _layer_norm(alpha * x_ref[...] + mix, g1_ref[...], b1_ref[...])
    x1b = x1.astype(BF16)
    d_ff = w1_ref.shape[1]
    acc = None
    for c in range(d_ff // tf):
        hid = jnp.maximum(_nn(x1b, w1_ref[:, c * tf:(c + 1) * tf]), 0.0)
        part = _nn((hid * hid).astype(BF16), w2_ref[c * tf:(c + 1) * tf, :])
        acc = part if acc is None else acc + part
    o_ref[...] = _layer_norm(alpha * x1 + acc, g2_ref[...], b2_ref[...])


def _post(x, acts, ws, g1, b1, g2, b2, w1, w2, *, tm, alpha, t_out=None):
    bx, tx, d = x.shape
    t_out = tx if t_out is None else t_out
    row = lambda b, i: (b, i, 0)
    tf = min(512, w1.shape[1])
    nbytes = (4 * tm * d * 4 + sum(2 * tm * a.shape[2] * 2 for a in acts) + sum(w.size * 2 for w in ws)
              + (w1.size + w2.size) * 2 + 6 * tm * d * 4 + 4 * tm * tf * 4 + (8 << 20))
    return pl.pallas_call(
        functools.partial(_post_kernel, n_act=len(acts), alpha=alpha, tf=tf),
        out_shape=jax.ShapeDtypeStruct((bx, t_out, d), F32),
        grid=(bx, t_out // tm),
        in_specs=[pl.BlockSpec((None, tm, d), row)]
                 + [pl.BlockSpec((None, tm, a.shape[2]), row) for a in acts]
                 + [_const_spec(w) for w in ws]
                 + [_const_spec(v) for v in (g1, b1, g2, b2, w1, w2)],
        out_specs=pl.BlockSpec((None, tm, d), row),
        compiler_params=_params(("parallel", "parallel"), nbytes),
        name="out_proj_mlp",
    )(x, *acts, *ws, g1, b1, g2, b2, w1, w2)


def _log_sigmoid(x):
    return -(jnp.maximum(-x, 0.0) + jnp.log1p(jnp.exp(-jnp.abs(x))))


def _cd_in_kernel(x_ref, cos_ref, sin_ref, w_ref, gb_ref, nrm_ref,
                  rq_ref, rk_ref, rv_ref, rg_ref, mq_ref, mk_ref, mv_ref, mg_ref, gt_ref,
                  *, qw, vw, n_ml, rk_scale, mk_scale):
    xb = x_ref[...].astype(BF16)
    cos = cos_ref[...]
    sin = sin_ref[...]

    def mm(lo, width):
        return _nn(xb, w_ref[:, lo:lo + width])

    rq_ref[...] = mm(0, qw) * cos + mm(qw, qw) * sin
    rk_ref[...] = (mm(2 * qw, qw) * cos + mm(3 * qw, qw) * sin) * rk_scale
    o = 4 * qw
    rv_ref[...] = mm(o, vw).astype(BF16)
    rg = mm(o + vw, vw)
    rg_ref[...] = rg * jax.nn.sigmoid(rg)
    o = o + 2 * vw
    mq_ref[...] = mm(o, qw)
    mk_ref[...] = mm(o + qw, qw) * mk_scale
    o = o + 2 * qw
    mv_ref[...] = mm(o, vw).astype(BF16)
    mg_ref[...] = jax.nn.sigmoid(mm(o + vw, vw)) * nrm_ref[...]
    pre = mm(o + 2 * vw, LANES) + gb_ref[...]
    lane = lax.broadcasted_iota(jnp.int32, pre.shape, 1)
    gt_ref[...] = jnp.where(lane < n_ml, pre, _log_sigmoid(pre))


def _cd_in(x, cos, sin, w_aug, gate_bias, ml_norm, *, tm, dims):
    bx, tx, d = x.shape
    qw, vw = dims["cd_qw"], dims["cd_vw"]
    row = lambda b, i: (b, i, 0)
    tab = lambda b, i: (i, 0)
    shapes = [(qw, F32), (qw, F32), (vw, BF16), (vw, F32), (qw, F32), (qw, F32), (vw, BF16), (vw, F32), (LANES, F32)]
    nbytes = (2 * tm * d * 4 + w_aug.size * 2 + 2 * tm * sum(w * (4 if t == F32 else 2) for w, t in shapes)
              + 8 * tm * vw * 4 + (8 << 20))
    return pl.pallas_call(
        functools.partial(_cd_in_kernel, qw=qw, vw=vw, n_ml=dims["n_ml"],
                          rk_scale=dims["rk_scale"], mk_scale=dims["mk_scale"]),
        out_shape=tuple(jax.ShapeDtypeStruct((bx, tx, w), t) for w, t in shapes),
        grid=(bx, tx // tm),
        in_specs=[pl.BlockSpec((None, tm, d), row),
                  pl.BlockSpec((tm, qw), tab), pl.BlockSpec((tm, qw), tab),
                  _const_spec(w_aug), _const_spec(gate_bias), _const_spec(ml_norm)],
        out_specs=tuple(pl.BlockSpec((None, tm, w), row) for w, _ in shapes),
        compiler_params=_params(("parallel", "parallel"), nbytes),
        name="cd_in_proj",
    )(x, cos, sin, w_aug, gate_bias, ml_norm)


def _split_bf16(x):
    hi = x.astype(BF16)
    lo = (x - hi.astype(F32)).astype(BF16)
    return hi, lo


def _chunk_kernel(ids_ref, rq_ref, rk_ref, rv_ref, rg_ref, mq_ref, mk_ref, mv_ref, mg_ref, gt_ref,
                  dec_ref, lgr_ref, lgc_ref, s0_ref, c0_ref, n0_ref, m0_ref,
                  o_ref, s_out, c_out, n_out, m_out,
                  s_sc, c_sc, n_sc, m_sc,
                  *, n_heads, dk, dv, first_valid):
    del ids_ref
    ci = pl.program_id(1)
    n_valid = jnp.where(ci == 0, first_valid, MB)
    qw = n_heads * dk

    @pl.when(ci == 0)
    def _load_state():
        s_sc[...] = s0_ref[...]
        c_sc[...] = c0_ref[...]
        n_sc[...] = n0_ref[...]
        m_sc[...] = m0_ref[...]

    idx_c = lax.broadcasted_iota(jnp.int32, (MB, 1), 0)
    live_c = idx_c < n_valid
    lane_q = lax.broadcasted_iota(jnp.int32, (MB, qw), 1)
    t_f = idx_c.astype(F32)
    nv_f = n_valid.astype(F32)

    lgr = lgr_ref[...]
    rq = rq_ref[...]
    rk = jnp.where(live_c, rk_ref[...], 0.0)
    rv = rv_ref[...]
    q_dec = rq * jnp.exp(lgr * (t_f + 1.0))
    k_dec = (rk * jnp.exp(lgr * jnp.where(live_c, nv_f - 1.0 - t_f, 0.0))).astype(BF16)
    rk_b = rk.astype(BF16)
    s_all = s_sc[...]
    s_b = s_all.astype(BF16)
    upd = _tn(k_dec, rv)
    s_dec = jnp.exp(lgc_ref[...] * nv_f)
    for h in range(n_heads):
        head = (lane_q >= h * dk) & (lane_q < (h + 1) * dk)
        inner = _nt(jnp.where(head, rq, 0.0).astype(BF16), rk_b) * dec_ref[h]
        o = _nn(inner.astype(BF16), rv[:, h * dv:(h + 1) * dv]) + _nn(jnp.where(head, q_dec, 0.0).astype(BF16), s_b)
        o = _head_norm(o) * rg_ref[:, h * dv:(h + 1) * dv]
        o_ref[:, h * dv:(h + 1) * dv] = o.astype(o_ref.dtype)
        s_sc[h * dk:(h + 1) * dk, :] = (s_dec[h * dk:(h + 1) * dk] * s_all[h * dk:(h + 1) * dk]
                                        + upd[h * dk:(h + 1) * dk, h * dv:(h + 1) * dv])

    gt = gt_ref[...]
    lane_g = lax.broadcasted_iota(jnp.int32, gt.shape, 1)
    gt = jnp.where(live_c, gt, jnp.where(lane_g < n_heads, NEG_INF, 0.0))
    row_i = lax.broadcasted_iota(jnp.int32, (MB, MB), 0)
    col_i = lax.broadcasted_iota(jnp.int32, (MB, MB), 1)
    causal = col_i <= row_i
    tri = causal.astype(BF16)
    g_hi, g_lo = _split_bf16(gt)
    cum_c = _nn(tri, g_hi) + _nn(tri, g_lo)
    gt_r = gt.T
    cum_r = cum_c.T
    mq = mq_ref[...]
    mk = jnp.where(live_c, mk_ref[...], 0.0)
    mk_b = mk.astype(BF16)
    mv = mv_ref[...]
    c_cat = c_sc[...]
    c_b = c_cat.astype(BF16)
    n_cat = n_sc[...]
    m_all = m_sc[...]
    c_new = jnp.zeros_like(c_cat)
    n_new = jnp.zeros_like(n_cat)
    g_keep = jnp.zeros_like(n_cat)
    m_new = jnp.zeros_like(m_all)
    lane_m = lax.broadcasted_iota(jnp.int32, m_all.shape, 1)
    lane_n = lax.broadcasted_iota(jnp.int32, n_cat.shape, 1)
    for h in range(n_heads):
        head = (lane_q >= h * dk) & (lane_q < (h + 1) * dk)
        head_n = (lane_n >= h * dk) & (lane_n < (h + 1) * dk)
        b_c = cum_c[:, n_heads + h:n_heads + h + 1]
        i_c = gt[:, h:h + 1]
        b_r = cum_r[n_heads + h:n_heads + h + 1, :]
        i_r = gt_r[h:h + 1, :]
        m_prev = m_all[:, h:h + 1]
        d = jnp.where(causal, b_c - b_r + i_r, NEG_INF)
        g = b_c + m_prev
        m_t = jnp.maximum(g, jnp.max(d, axis=1, keepdims=True))
        w = jnp.exp(d - m_t)
        gs = jnp.exp(g - m_t)
        qh = jnp.where(head, mq, 0.0)
        qh_b = qh.astype(BF16)
        a = _nt(qh_b, mk_b) * w
        num = _nn(a.astype(BF16), mv[:, h * dv:(h + 1) * dv]) + gs * _nt(qh_b, c_b)
        den = jnp.sum(a, axis=1, keepdims=True) + gs * jnp.sum(qh * n_cat, axis=1, keepdims=True)
        hid = num / jnp.maximum(jnp.abs(den), jnp.exp(-m_t))
        hid = _head_norm(hid) * mg_ref[:, h * dv:(h + 1) * dv]
        o_ref[:, n_heads * dv + h * dv:n_heads * dv + (h + 1) * dv] = hid.astype(o_ref.dtype)
        m_last = m_t[MB - 1:MB]
        w_last = jnp.exp(b_c[MB - 1:MB] - b_c + i_c - m_last)
        g_last = gs[MB - 1:MB]
        kh = jnp.where(head, mk, 0.0)
        c_new = c_new + _tn((mv[:, h * dv:(h + 1) * dv].astype(F32) * w_last).astype(BF16), kh.astype(BF16))
        n_new = n_new + jnp.sum(kh * w_last, axis=0, keepdims=True)
        g_keep = jnp.where(head_n, g_last, g_keep)
        m_new = jnp.where(lane_m == h, m_last, m_new)
    c_sc[...] = g_keep * c_cat + c_new
    n_sc[...] = g_keep * n_cat + n_new
    m_sc[...] = m_new

    @pl.when(ci == pl.num_programs(1) - 1)
    def _store_state():
        s_out[...] = s_sc[...]
        c_out[...] = c_sc[...]
        n_out[...] = n_sc[...]
        m_out[...] = m_sc[...]


def _chunk_scan(rq, rk, rv, rg, mq, mk, mv, mg, gt, dec, lgr, lgc, s0, c0, n0, m0,
                *, n_heads, dk, dv, blocks, first_valid):
    bx, tx, qw = rq.shape
    vw = rv.shape[2]
    blk_ids = jnp.asarray(blocks, jnp.int32)
    row = lambda b, c, ids: (b, ids[c], 0)
    st = lambda b, c, ids: (b, 0, 0)
    cst3 = lambda b, c, ids: (0, 0, 0)
    cst2 = lambda b, c, ids: (0, 0)
    state_shapes = [(qw, dv), (dv, qw), (1, qw), (1, LANES)]
    grid_spec = pltpu.PrefetchScalarGridSpec(
        num_scalar_prefetch=1,
        grid=(bx, len(blocks)),
        in_specs=[pl.BlockSpec((None, MB, qw), row), pl.BlockSpec((None, MB, qw), row),
                  pl.BlockSpec((None, MB, vw), row), pl.BlockSpec((None, MB, vw), row),
                  pl.BlockSpec((None, MB, qw), row), pl.BlockSpec((None, MB, qw), row),
                  pl.BlockSpec((None, MB, vw), row), pl.BlockSpec((None, MB, vw), row),
                  pl.BlockSpec((None, MB, LANES), row),
                  pl.BlockSpec(dec.shape, cst3), pl.BlockSpec(lgr.shape, cst2), pl.BlockSpec(lgc.shape, cst2)]
                 + [pl.BlockSpec((None,) + s, st) for s in state_shapes],
        out_specs=[pl.BlockSpec((None, MB, 2 * vw), row)] + [pl.BlockSpec((None,) + s, st) for s in state_shapes],
        scratch_shapes=[pltpu.VMEM(s, F32) for s in state_shapes],
    )
    out_shape = [jax.ShapeDtypeStruct((bx, tx, 2 * vw), BF16)] + [jax.ShapeDtypeStruct((bx,) + s, F32) for s in state_shapes]
    return pl.pallas_call(
        functools.partial(_chunk_kernel, n_heads=n_heads, dk=dk, dv=dv, first_valid=first_valid),
        out_shape=out_shape,
        grid_spec=grid_spec,
        compiler_params=_params(("parallel", "arbitrary"), 32 << 20),
        name="retention_mlstm_chunks",
    )(blk_ids, rq, rk, rv, rg, mq, mk, mv, mg, gt, dec, lgr, lgc, s0, c0, n0, m0)


def _rope_tables(pos, half, reps, width):
    freqs = ROPE_BASE ** (-jnp.arange(half, dtype=F32) / half)
    ang = pos.astype(F32)[:, None] * freqs[None, :]
    cos, sin = jnp.cos(ang), jnp.sin(ang)
    cos = jnp.tile(jnp.concatenate([cos, cos], axis=1), (1, reps))
    sin = jnp.tile(jnp.concatenate([-sin, sin], axis=1), (1, reps))
    pad = width - cos.shape[1]
    return jnp.pad(cos, ((0, 0), (0, pad))), jnp.pad(sin, ((0, 0), (0, pad)))


def _swap_halves(w, group):
    shp = w.shape
    w = w.reshape(shp[:-1] + (shp[-1] // group, 2, group // 2))
    return w[..., ::-1, :].reshape(shp)


def _pad_cols(w, width):
    return jnp.pad(w, ((0, 0), (0, width - w.shape[1])))


def _pad_page(rows, n_seq, t_new, page):
    w = rows.shape[-1]
    return jnp.pad(rows.reshape(n_seq, t_new, w), ((0, 0), (0, page - t_new), (0, 0)))


def _meta_first(a, t_real, n_meta):
    return jnp.concatenate([a[:, t_real:t_real + n_meta], a[:, :t_real]], axis=1)


def kernel(x_prompt, x_sample, cache_mla_latent, cache_mla_rope, cache_diff_k, cache_diff_v, state_ret, state_mlstm_C, state_mlstm_n, state_mlstm_m, page_table, meta_tokens, ab_w_in, mla_q_norm, mla_w_uq, mla_kv_norm, mla_w_uk, mla_w_uv, diff_lambda_q1, diff_lambda_k1, diff_lambda_q2, diff_lambda_k2, diff_subln, ab_w_out, cd_w_in, ml_b_i, ml_b_f, ml_norm, cd_w_out, ln1_g, ln1_b, ln2_g, ln2_b, mlp_w1, mlp_w2):
    b, t_real, d = x_prompt.shape
    n_seq, t_new, _ = x_sample.shape
    n_meta = meta_tokens.shape[0]
    depth = ln1_g.shape[0]
    alpha = (2 * depth) ** 0.25
    page = cache_mla_latent.shape[2]
    past_len = page_table.shape[1] * page
    ql, kl, rope = mla_q_norm.shape[1], mla_kv_norm.shape[1], cache_mla_rope.shape[3]
    n_mla, nope, mla_v = mla_w_uk.shape[2], mla_w_uk.shape[3], mla_w_uv.shape[3]
    n_diff, diff_qk, diff_v = cache_diff_k.shape[3], cache_diff_k.shape[5], cache_diff_v.shape[4]
    n_ret, ret_dk, ret_dv = state_ret.shape[2], state_ret.shape[3], state_ret.shape[4]
    n_ml, ml_dv, ml_dk = state_mlstm_C.shape[2], state_mlstm_C.shape[3], state_mlstm_C.shape[4]
    assert 2 * diff_qk == LANES and diff_v == LANES and n_meta <= MB and t_real % MB == 0 and page == MB
    assert (n_ret, ret_dk, ret_dv) == (n_ml, ml_dk, ml_dv) and ret_dv == LANES and t_new <= MB
    dw = n_diff * LANES
    dims = dict(ql=ql, kl=kl, dw=dw, rope=rope, n_mla=n_mla, n_diff=n_diff, dq_scale=diff_qk ** -0.5,
                cd_qw=n_ret * ret_dk, cd_vw=n_ret * ret_dv, n_ml=n_ml, rk_scale=ret_dk ** -0.5, mk_scale=ml_dk ** -0.5)
    tp = t_real + MB
    n_rows_s = n_seq * t_new
    tm_p = _tile(tp, 640)
    tm_s = _tile(n_rows_s, 512)
    tq = _tile(t_real, 512)

    xp = jnp.concatenate([x_prompt, jnp.broadcast_to(meta_tokens.astype(x_prompt.dtype)[None], (b, n_meta, d)),
                          jnp.zeros((b, MB - n_meta, d), x_prompt.dtype)], axis=1)
    xs = x_sample.reshape(1, n_rows_s, d)
    pos_p = jnp.concatenate([n_meta + jnp.arange(t_real), jnp.arange(MB)])
    pos_s = past_len + (jnp.arange(n_rows_s) % t_new)

    prompt_out, sample_out = {}, {}
    for layer in range(depth):
        g1, b1, g2, b2 = (v[layer][None] for v in (ln1_g, ln1_b, ln2_g, ln2_b))
        w1, w2 = mlp_w1[layer].astype(BF16), mlp_w2[layer].astype(BF16)
        last = layer == depth - 1
        if layer % 2 == 0:
            li = layer // 2
            lambda_init = 0.8 - 0.6 * math.exp(-0.3 * layer)
            scale = (nope + rope) ** -0.5
            w_in = ab_w_in[li]
            o = [0, ql, ql + kl, ql + kl + rope, ql + kl + rope + dw, ql + kl + rope + 2 * dw, ql + kl + rope + 3 * dw]
            w_cq, w_ckv, w_kr, w_dq, w_dk, w_dv = (w_in[:, o[i]:o[i + 1]] for i in range(6))
            w_aug = jnp.concatenate([w_cq, w_ckv, w_dq, w_dk, w_dv, _pad_cols(w_kr, LANES),
                                     _pad_cols(_swap_halves(w_kr, rope), LANES)], axis=1).astype(BF16)
            w_uq = mla_w_uq[li].reshape(ql, n_mla, nope + rope)
            w_abs = _wprod(w_uq[:, :, :nope].transpose(1, 0, 2), mla_w_uk[li].transpose(1, 0, 2), True)
            w_pe = w_uq[:, :, nope:].transpose(1, 0, 2)
            pe_pad = ((0, 0), (0, 0), (0, LANES - rope))
            wq_big = (jnp.concatenate([w_abs, jnp.pad(w_pe, pe_pad), jnp.pad(_swap_halves(w_pe, rope), pe_pad)], axis=2)
                      * scale).transpose(1, 0, 2).reshape(ql, n_mla * (kl + 2 * LANES)).astype(BF16)
            w_out = ab_w_out[li]
            w_comb = _wprod(mla_w_uv[li].transpose(1, 0, 2), w_out[:n_mla * mla_v].reshape(n_mla, mla_v, d), False)
            w_comb = w_comb.reshape(n_mla * kl, d).astype(BF16)
            w_out_diff = w_out[n_mla * mla_v:].astype(BF16)
            qn, kvn = mla_q_norm[li][None], mla_kv_norm[li][None]
            lam = jnp.stack([diff_lambda_q1[li], diff_lambda_k1[li], diff_lambda_q2[li], diff_lambda_k2[li]])
            subln = diff_subln[li][None]

            cos_p, sin_p = _rope_tables(pos_p, rope // 2, 1, LANES)
            cos_s, sin_s = _rope_tables(pos_s, rope // 2, 1, LANES)
            c_p, kr_p, kcat_p, q_p, dq_p, dk_p, dv_p, dkb_p, dvb_p = _ab_in(
                xp, cos_p, sin_p, w_aug, wq_big, qn, kvn, tm=tm_p, dims=dims)
            c_s, kr_s, _, q_s, dq_s, dk_s, dv_s, _, _ = _ab_in(
                xs, cos_s, sin_s, w_aug, wq_big, qn, kvn, tm=tm_s, dims=dims)

            o_lat_p = _mla_prompt(q_p, kcat_p, t_real=t_real, n_meta=n_meta, tq=tq)
            o_diff_p = _diff_prompt(dq_p, dkb_p, dvb_p, lam, subln, t_real=t_real, n_meta=n_meta, tq=tq,
                                    qk=diff_qk, lambda_init=lambda_init)

            kw = kl + LANES
            q_dec = q_s.reshape(n_mla, n_seq, t_new, kw).transpose(1, 0, 2, 3).reshape(n_seq, n_mla * t_new, kw)
            o_lat_s = _mla_decode(q_dec, _pad_page(c_s[0], n_seq, t_new, page), _pad_page(kr_s[0], n_seq, t_new, page),
                                  cache_mla_latent, cache_mla_rope, page_table, li,
                                  t_new=t_new, pages_per_step=_tile_pages(page_table.shape[1], 16))
            o_lat_s = (o_lat_s.reshape(n_seq, n_mla, t_new, kl).transpose(0, 2, 1, 3)
                       .reshape(1, n_rows_s, n_mla * kl))
            dq5 = dq_s.reshape(n_diff, n_seq, t_new, 2, diff_qk).transpose(1, 0, 3, 2, 4)
            eye_h = jnp.eye(n_diff, dtype=BF16)
            eye_m = jnp.eye(2, dtype=BF16)
            q_bd = (dq5[:, :, :, :, None, None, :] * eye_h[None, :, None, None, :, None, None]
                    * eye_m[None, None, :, None, None, :, None]).reshape(n_seq, n_diff * 2 * t_new, dw)
            k_pool = cache_diff_k.reshape(cache_diff_k.shape[:3] + (dw,))
            v_pool = cache_diff_v.reshape(cache_diff_v.shape[:3] + (dw,))
            o_diff_s = _diff_decode(q_bd, _pad_page(dk_s[0], n_seq, t_new, page), _pad_page(dv_s[0], n_seq, t_new, page),
                                    lam, subln, k_pool, v_pool, page_table, li, t_new=t_new, n_heads=n_diff,
                                    pages_per_step=_tile_pages(page_table.shape[1], 8), lambda_init=lambda_init)
            o_diff_s = (o_diff_s.reshape(n_seq, n_diff, 2, t_new, LANES)[:, :, 0].transpose(0, 2, 1, 3)
                        .reshape(1, n_rows_s, dw).astype(BF16))

            acts_p, acts_s, ws = [o_lat_p, o_diff_p], [o_lat_s, o_diff_s], [w_comb, w_out_diff]
            for name, rows_p, rows_s in (("c", c_p, c_s), ("kr", kr_p, kr_s), ("dk", dk_p, dk_s), ("dv", dv_p, dv_s)):
                prompt_out.setdefault(name, []).append(_meta_first(rows_p, t_real, n_meta))
                sample_out.setdefault(name, []).append(rows_s.reshape(n_seq, t_new, -1))
        else:
            lj = layer // 2
            qw, vw = dims["cd_qw"], dims["cd_vw"]
            w_in = cd_w_in[lj]
            widths = [qw, qw, vw, vw, qw, qw, vw, vw, n_ml, n_ml]
            offs = [0]
            for wdt in widths:
                offs.append(offs[-1] + wdt)
            w_rq, w_rk, w_rv, w_rg, w_mq, w_mk, w_mv, w_mo, w_mi, w_mf = (w_in[:, offs[i]:offs[i + 1]] for i in range(10))
            w_aug = jnp.concatenate([w_rq, _swap_halves(w_rq, ret_dk), w_rk, _swap_halves(w_rk, ret_dk), w_rv, w_rg,
                                     w_mq, w_mk, w_mv, w_mo,
                                     _pad_cols(jnp.concatenate([w_mi, w_mf], axis=1), LANES)], axis=1).astype(BF16)
            gate_bias = _pad_cols(jnp.concatenate([ml_b_i[lj], ml_b_f[lj]])[None], LANES)
            nrm = ml_norm[lj][None]
            w_out = cd_w_out[lj].astype(BF16)
            log_gamma = jnp.log1p(-jnp.power(2.0, -5.0 - jnp.arange(n_ret, dtype=F32)))
            idx = jnp.arange(MB, dtype=F32)
            diff_ts = idx[:, None] - idx[None, :]
            dec = jnp.where(diff_ts >= 0, jnp.exp(log_gamma[:, None, None] * jnp.maximum(diff_ts, 0.0)), 0.0)
            lgr = jnp.repeat(log_gamma, ret_dk)[None]
            lgc = jnp.repeat(log_gamma, ret_dk)[:, None]

            cos_p, sin_p = _rope_tables(pos_p, ret_dk // 2, n_ret, qw)
            cos_s, sin_s = _rope_tables(pos_s, ret_dk // 2, n_ret, qw)
            rows_p = _cd_in(xp, cos_p, sin_p, w_aug, gate_bias, nrm, tm=tm_p, dims=dims)
            rows_s = _cd_in(xs, cos_s, sin_s, w_aug, gate_bias, nrm, tm=tm_s, dims=dims)

            zeros = lambda *s: jnp.zeros(s, F32)
            chunk = functools.partial(_chunk_scan, n_heads=n_ret, dk=ret_dk, dv=ret_dv)
            o_p, s_p, c_p2, n_p, m_p = chunk(
                *rows_p, dec, lgr, lgc, zeros(b, qw, ret_dv), zeros(b, ret_dv, qw), zeros(b, 1, qw), zeros(b, 1, LANES),
                blocks=[t_real // MB] + list(range(t_real // MB)), first_valid=n_meta)
            rows_s_pad = [_pad_page(r[0], n_seq, t_new, MB) for r in rows_s]
            c0 = state_mlstm_C[lj].transpose(0, 2, 1, 3).reshape(n_seq, ml_dv, qw)
            o_s, s_s, c_s2, n_s, m_s = chunk(
                *rows_s_pad, dec, lgr, lgc, state_ret[lj].reshape(n_seq, qw, ret_dv), c0,
                state_mlstm_n[lj].reshape(n_seq, 1, qw), _pad_cols(state_mlstm_m[lj], LANES)[:, None],
                blocks=[0], first_valid=t_new)
            o_s = o_s[:, :t_new].reshape(1, n_rows_s, 2 * vw)

            acts_p, acts_s, ws = [o_p], [o_s], [w_out]
            for dst, s_, c_, n_, m_, nb in ((prompt_out, s_p, c_p2, n_p, m_p, b), (sample_out, s_s, c_s2, n_s, m_s, n_seq)):
                dst.setdefault("s", []).append(s_.reshape(nb, n_ret, ret_dk, ret_dv))
                dst.setdefault("C", []).append(c_.reshape(nb, ml_dv, n_ml, ml_dk).transpose(0, 2, 1, 3))
                dst.setdefault("n", []).append(n_.reshape(nb, n_ml, ml_dk))
                dst.setdefault("m", []).append(m_[:, 0, :n_ml])

        if last:
            xp = _post(xp, acts_p, ws, g1, b1, g2, b2, w1, w2, tm=_tile(t_real, 512), alpha=alpha, t_out=t_real)
        else:
            xp = _post(xp, acts_p, ws, g1, b1, g2, b2, w1, w2, tm=tm_p, alpha=alpha)
        xs = _post(xs, acts_s, ws, g1, b1, g2, b2, w1, w2, tm=tm_s, alpha=alpha)

    y_prompt = xp
    y_sample = xs.reshape(n_seq, t_new, d)
    k_shape_p = (b, n_meta + t_real, n_diff, 2, diff_qk)
    v_shape_p = (b, n_meta + t_real, n_diff, diff_v)
    k_shape_s = (n_seq, t_new, n_diff, 2, diff_qk)
    v_shape_s = (n_seq, t_new, n_diff, diff_v)
    return (y_prompt, y_sample,
            jnp.stack(prompt_out["c"]), jnp.stack(prompt_out["kr"]),
            jnp.stack([a.reshape(k_shape_p) for a in prompt_out["dk"]]),
            jnp.stack([a.reshape(v_shape_p) for a in prompt_out["dv"]]),
            jnp.stack(prompt_out["s"]), jnp.stack(prompt_out["C"]), jnp.stack(prompt_out["n"]), jnp.stack(prompt_out["m"]),
            jnp.stack(sample_out["c"]), jnp.stack(sample_out["kr"]),
            jnp.stack([a.reshape(k_shape_s) for a in sample_out["dk"]]),
            jnp.stack([a.reshape(v_shape_s) for a in sample_out["dv"]]),
            jnp.stack(sample_out["s"]), jnp.stack(sample_out["C"]), jnp.stack(sample_out["n"]), jnp.stack(sample_out["m"]))


def _tile_pages(n_pages, pref):
    g = min(pref, n_pages)
    while n_pages % g:
        g -= 1
    return g
```

```python
import functools
import math

import jax
import jax.numpy as jnp
from jax import lax
from jax.experimental import pallas as pl
from jax.experimental.pallas import tpu as pltpu

F32 = jnp.float32
BF16 = jnp.bfloat16

ROPE_BASE = 10000.0
NEG_INF = -1e30
EPS = 1e-5

LANES = 128
MB = 128
HEAD_UNROLL = 4
VMEM_CAP_BYTES = 56 << 20


def _vmem(nbytes):
    return int(min(VMEM_CAP_BYTES, max(16 << 20, nbytes)))


def _params(sem, nbytes):
    return pltpu.CompilerParams(dimension_semantics=sem, vmem_limit_bytes=_vmem(nbytes))


def _const_spec(arr):
    nd = arr.ndim
    return pl.BlockSpec(arr.shape, lambda *_: (0,) * nd, pipeline_mode=pl.Buffered(1))


def _tile(n, pref):
    if n <= pref:
        return n
    best = MB
    for t in range(MB, pref + 1, MB):
        if n % t == 0:
            best = t
    assert n % best == 0, (n, pref)
    return best


def _nt(a, b):
    return lax.dot_general(a, b, (((1,), (1,)), ((), ())), preferred_element_type=F32)


def _nn(a, b):
    return jnp.dot(a, b, preferred_element_type=F32)


def _tn(a, b):
    return _nn(a.T, b)


def _rms(x):
    return x * lax.rsqrt(jnp.mean(x * x, axis=-1, keepdims=True) + EPS)


def _layer_norm(x, g, b):
    mu = jnp.mean(x, axis=-1, keepdims=True)
    xc = x - mu
    var = jnp.mean(xc * xc, axis=-1, keepdims=True)
    return xc * lax.rsqrt(var + EPS) * g + b


def _head_norm(x):
    mu = jnp.mean(x, axis=-1, keepdims=True)
    xc = x - mu
    var = jnp.mean(xc * xc, axis=-1, keepdims=True)
    return xc * lax.rsqrt(var + EPS)


LOG2E = math.log2(math.e)


def _lane_tile(x, width):
    reps = width // LANES
    return x if reps == 1 else jnp.concatenate([x] * reps, axis=1)


def _lane_fold(p):
    part = p[:, :LANES]
    for i in range(1, p.shape[1] // LANES):
        part = part + p[:, i * LANES:(i + 1) * LANES]
    return part


def _softmax_first(s, m_ref, l_ref, idx):
    m = jnp.broadcast_to(jnp.max(s, axis=1, keepdims=True), (s.shape[0], LANES))
    p = jnp.exp2(s - _lane_tile(m, s.shape[1]))
    m_ref[idx] = m
    l_ref[idx] = _lane_fold(p)
    return p


def _softmax_next(s, m_ref, l_ref, idx):
    m_prev = m_ref[idx]
    m_new = jnp.maximum(m_prev, jnp.max(s, axis=1, keepdims=True))
    alpha = jnp.exp2(m_prev - m_new)
    p = jnp.exp2(s - _lane_tile(m_new, s.shape[1]))
    l_ref[idx] = alpha * l_ref[idx] + _lane_fold(p)
    m_ref[idx] = m_new
    return alpha, p


def _softmax_total(l_ref, idx):
    return jnp.sum(l_ref[idx], axis=1, keepdims=True)


def _wprod_kernel(a_ref, b_ref, o_ref, *, trans_b):
    a = a_ref[...].astype(BF16)
    b = b_ref[...].astype(BF16)
    o_ref[...] = _nt(a, b) if trans_b else _nn(a, b)


def _wprod(a, b, trans_b):
    h, m, _ = a.shape
    n = b.shape[1] if trans_b else b.shape[2]
    return pl.pallas_call(
        functools.partial(_wprod_kernel, trans_b=trans_b),
        out_shape=jax.ShapeDtypeStruct((h, m, n), F32),
        grid=(h,),
        in_specs=[pl.BlockSpec((None,) + a.shape[1:], lambda i: (i, 0, 0)),
                  pl.BlockSpec((None,) + b.shape[1:], lambda i: (i, 0, 0))],
        out_specs=pl.BlockSpec((None, m, n), lambda i: (i, 0, 0)),
        compiler_params=_params(("arbitrary",), 16 << 20),
        name="weight_product",
    )(a, b)


def _ab_in_kernel(x_ref, cos_ref, sin_ref, w_ref, wq_ref, qn_ref, kvn_ref,
                  c_ref, kr_ref, kcat_ref, q_ref, dq_ref, dk_ref, dv_ref, dkb_ref, dvb_ref,
                  *, ql, kl, dw, rope, n_mla, n_diff, dq_scale):
    xb = x_ref[...].astype(BF16)
    cos = cos_ref[...]
    sin = sin_ref[...]

    def mm(lo, width):
        return _nn(xb, w_ref[:, lo:lo + width])

    o_ckv, o_dq, o_dk, o_dv, o_kr = ql, ql + kl, ql + kl + dw, ql + kl + 2 * dw, ql + kl + 3 * dw
    c = _rms(mm(o_ckv, kl)) * kvn_ref[...]
    krr = mm(o_kr, 2 * LANES)
    kr = krr[:, :LANES] * cos + krr[:, LANES:] * sin
    c_ref[...] = c
    kr_ref[...] = kr[:, :rope]
    kcat_ref[:, :kl] = c.astype(BF16)
    kcat_ref[:, kl:] = kr.astype(BF16)
    dq = mm(o_dq, dw) * dq_scale
    dk = mm(o_dk, dw)
    dv = mm(o_dv, dw)
    dk_ref[...] = dk
    dv_ref[...] = dv
    for h in range(n_diff):
        sl = slice(h * LANES, (h + 1) * LANES)
        dq_ref[h] = dq[:, sl].astype(BF16)
        dkb_ref[h] = dk[:, sl].astype(BF16)
        dvb_ref[h] = dv[:, sl].astype(BF16)
    cq = (_rms(mm(0, ql)) * qn_ref[...]).astype(BF16)
    hw = kl + 2 * LANES
    for h in range(n_mla):
        qh = _nn(cq, wq_ref[:, h * hw:(h + 1) * hw])
        q_ref[h, :, :kl] = qh[:, :kl].astype(BF16)
        q_ref[h, :, kl:] = (qh[:, kl:kl + LANES] * cos + qh[:, kl + LANES:] * sin).astype(BF16)


def _ab_in(x, cos, sin, w_aug, wq_big, q_norm, kv_norm, *, tm, dims):
    bx, tx, d = x.shape
    ql, kl, dw, rope, n_mla, n_diff = dims["ql"], dims["kl"], dims["dw"], dims["rope"], dims["n_mla"], dims["n_diff"]
    row = lambda b, i: (b, i, 0)
    hrow = lambda b, i: (b, 0, i, 0)
    tab = lambda b, i: (i, 0)
    out_shape = (
        jax.ShapeDtypeStruct((bx, tx, kl), F32),
        jax.ShapeDtypeStruct((bx, tx, rope), F32),
        jax.ShapeDtypeStruct((bx, tx, kl + LANES), BF16),
        jax.ShapeDtypeStruct((bx, n_mla, tx, kl + LANES), BF16),
        jax.ShapeDtypeStruct((bx, n_diff, tx, LANES), BF16),
        jax.ShapeDtypeStruct((bx, tx, dw), F32),
        jax.ShapeDtypeStruct((bx, tx, dw), F32),
        jax.ShapeDtypeStruct((bx, n_diff, tx, LANES), BF16),
        jax.ShapeDtypeStruct((bx, n_diff, tx, LANES), BF16),
    )
    out_specs = (
        pl.BlockSpec((None, tm, kl), row),
        pl.BlockSpec((None, tm, rope), row),
        pl.BlockSpec((None, tm, kl + LANES), row),
        pl.BlockSpec((None, n_mla, tm, kl + LANES), hrow),
        pl.BlockSpec((None, n_diff, tm, LANES), hrow),
        pl.BlockSpec((None, tm, dw), row),
        pl.BlockSpec((None, tm, dw), row),
        pl.BlockSpec((None, n_diff, tm, LANES), hrow),
        pl.BlockSpec((None, n_diff, tm, LANES), hrow),
    )
    nbytes = (2 * tm * d * 4 + w_aug.size * 2 + wq_big.size * 2
              + 2 * tm * (kl * 4 + LANES * 4 + (kl + LANES) * 2 + n_mla * (kl + LANES) * 2
                          + 3 * n_diff * LANES * 2 + 2 * dw * 4)
              + 6 * tm * dw * 4 + (8 << 20))
    return pl.pallas_call(
        functools.partial(_ab_in_kernel, ql=ql, kl=kl, dw=dw, rope=rope, n_mla=n_mla, n_diff=n_diff,
                          dq_scale=dims["dq_scale"]),
        out_shape=out_shape,
        grid=(bx, tx // tm),
        in_specs=[pl.BlockSpec((None, tm, d), row),
                  pl.BlockSpec((tm, LANES), tab),
                  pl.BlockSpec((tm, LANES), tab),
                  _const_spec(w_aug), _const_spec(wq_big), _const_spec(q_norm), _const_spec(kv_norm)],
        out_specs=out_specs,
        compiler_params=_params(("parallel", "parallel"), nbytes),
        name="ab_in_proj",
    )(x, cos, sin, w_aug, wq_big, q_norm, kv_norm)


def _tri_schedule(nq):
    qi = [i for i in range(nq) for _ in range(i + 1)]
    kj = [j for i in range(nq) for j in range(i + 1)]
    return jnp.asarray(qi, jnp.int32), jnp.asarray(kj, jnp.int32)


def _mla_prompt_kernel(qi_ref, kj_ref, q_ref, k_ref, km_ref, o_ref, m_sc, l_sc, acc_sc,
                       *, n_heads, kl, n_meta, tq):
    step = pl.program_id(1)
    qi = qi_ref[step]
    kj = kj_ref[step]

    @pl.when(kj == 0)
    def _init():
        km = km_ref[...]
        vm = km_ref[:, :kl]
        valid = lax.broadcasted_iota(jnp.int32, (tq, MB), 1) < n_meta

        def body(h, carry):
            p = _softmax_first(jnp.where(valid, _nt(q_ref[h], km), NEG_INF), m_sc, l_sc, h)
            acc_sc[h] = _nn(p.astype(BF16), vm)
            return carry

        lax.fori_loop(0, n_heads, body, 0)

    def block(visible):
        k = k_ref[...]
        v = k_ref[:, :kl]

        def body(i, carry):
            for u in range(HEAD_UNROLL):
                h = i * HEAD_UNROLL + u
                s = _nt(q_ref[h], k)
                if visible is not None:
                    s = jnp.where(visible, s, NEG_INF)
                alpha, p = _softmax_next(s, m_sc, l_sc, h)
                acc_sc[h] = _lane_tile(alpha, kl) * acc_sc[h] + _nn(p.astype(BF16), v)
            return carry

        lax.fori_loop(0, n_heads // HEAD_UNROLL, body, 0)

    @pl.when(kj < qi)
    def _below_diagonal():
        block(None)

    @pl.when(kj == qi)
    def _diagonal():
        row = lax.broadcasted_iota(jnp.int32, (tq, tq), 0)
        col = lax.broadcasted_iota(jnp.int32, (tq, tq), 1)
        block(col <= row)
        for h in range(n_heads):
            o_ref[:, h * kl:(h + 1) * kl] = (acc_sc[h] / _softmax_total(l_sc, h)).astype(o_ref.dtype)


def _mla_meta_kernel(q_ref, km_ref, prev_ref, o_ref, *, n_heads, kl, n_meta):
    del prev_ref
    km = km_ref[...]
    vm = km_ref[:, :kl]
    row = lax.broadcasted_iota(jnp.int32, (MB, MB), 0)
    col = lax.broadcasted_iota(jnp.int32, (MB, MB), 1)
    visible = (col <= row) & (col < n_meta)
    for h in range(n_heads):
        s = jnp.where(visible, _nt(q_ref[h], km), NEG_INF)
        p = jnp.exp2(s - jnp.max(s, axis=1, keepdims=True))
        o = _nn(p.astype(BF16), vm) / jnp.sum(p, axis=1, keepdims=True)
        o_ref[:, h * kl:(h + 1) * kl] = o.astype(o_ref.dtype)


def _mla_prompt(q, kcat, *, t_real, n_meta, tq):
    b, n_heads, tp, kw = q.shape
    kl = kw - LANES
    nq = t_real // tq
    qi, kj = _tri_schedule(nq)
    meta_blk = t_real // MB
    grid_spec = pltpu.PrefetchScalarGridSpec(
        num_scalar_prefetch=2,
        grid=(b, int(qi.shape[0])),
        in_specs=[pl.BlockSpec((None, n_heads, tq, kw), lambda bb, s, qi_r, kj_r: (bb, 0, qi_r[s], 0)),
                  pl.BlockSpec((None, tq, kw), lambda bb, s, qi_r, kj_r: (bb, kj_r[s], 0)),
                  pl.BlockSpec((None, MB, kw), lambda bb, s, qi_r, kj_r: (bb, meta_blk, 0))],
        out_specs=pl.BlockSpec((None, tq, n_heads * kl), lambda bb, s, qi_r, kj_r: (bb, qi_r[s], 0)),
        scratch_shapes=[pltpu.VMEM((n_heads, tq, LANES), F32), pltpu.VMEM((n_heads, tq, LANES), F32),
                        pltpu.VMEM((n_heads, tq, kl), F32)],
    )
    nbytes = (2 * n_heads * tq * kw * 2 + 2 * tq * kw * 2 + 2 * tq * n_heads * kl * 2
              + n_heads * tq * (kl + 2 * LANES) * 4 + 8 * tq * tq * 4 + (8 << 20))
    o_real = pl.pallas_call(
        functools.partial(_mla_prompt_kernel, n_heads=n_heads, kl=kl, n_meta=n_meta, tq=tq),
        out_shape=jax.ShapeDtypeStruct((b, tp, n_heads * kl), BF16),
        grid_spec=grid_spec,
        compiler_params=_params(("parallel", "arbitrary"), nbytes),
        name="mla_prompt_attention",
    )(qi, kj, q, kcat, kcat)
    return pl.pallas_call(
        functools.partial(_mla_meta_kernel, n_heads=n_heads, kl=kl, n_meta=n_meta),
        out_shape=jax.ShapeDtypeStruct((b, tp, n_heads * kl), BF16),
        grid=(b,),
        in_specs=[pl.BlockSpec((None, n_heads, MB, kw), lambda bb: (bb, 0, meta_blk, 0)),
                  pl.BlockSpec((None, MB, kw), lambda bb: (bb, meta_blk, 0)),
                  pl.BlockSpec(memory_space=pl.ANY)],
        out_specs=pl.BlockSpec((None, MB, n_heads * kl), lambda bb: (bb, meta_blk, 0)),
        input_output_aliases={2: 0},
        compiler_params=_params(("parallel",), 16 << 20),
        name="mla_meta_attention",
    )(q, kcat, o_real)


def _lambda_diff(lam_ref, lambda_init):
    lam = lam_ref[...]
    return (jnp.exp(jnp.sum(lam[0:1] * lam[1:2], axis=1, keepdims=True))
            - jnp.exp(jnp.sum(lam[2:3] * lam[3:4], axis=1, keepdims=True)) + lambda_init)


def _diff_split(q, qk):
    lane = lax.broadcasted_iota(jnp.int32, q.shape, 1)
    zero = jnp.zeros_like(q)
    return jnp.concatenate([jnp.where(lane < qk, q, zero), jnp.where(lane >= qk, q, zero)], axis=0)


def _diff_prompt_kernel(qi_ref, kj_ref, q_ref, k_ref, v_ref, km_ref, vm_ref, lam_ref, sub_ref,
                        o_ref, m_sc, l_sc, acc_sc, *, n_heads, qk, n_meta, tq, lambda_init):
    step = pl.program_id(1)
    qi = qi_ref[step]
    kj = kj_ref[step]

    @pl.when(kj == 0)
    def _init():
        valid = lax.broadcasted_iota(jnp.int32, (2 * tq, MB), 1) < n_meta

        def body(h, carry):
            s = jnp.where(valid, _nt(_diff_split(q_ref[h], qk), km_ref[h]), NEG_INF)
            p = _softmax_first(s, m_sc, l_sc, h)
            acc_sc[h] = _nn(p.astype(BF16), vm_ref[h])
            return carry

        lax.fori_loop(0, n_heads, body, 0)

    def block(visible):
        def body(i, carry):
            for u in range(HEAD_UNROLL):
                h = i * HEAD_UNROLL + u
                s = _nt(_diff_split(q_ref[h], qk), k_ref[h])
                if visible is not None:
                    s = jnp.where(visible, s, NEG_INF)
                alpha, p = _softmax_next(s, m_sc, l_sc, h)
                acc_sc[h] = alpha * acc_sc[h] + _nn(p.astype(BF16), v_ref[h])
            return carry

        lax.fori_loop(0, n_heads // HEAD_UNROLL, body, 0)

    @pl.when(kj < qi)
    def _below_diagonal():
        block(None)

    @pl.when(kj == qi)
    def _diagonal():
        row = lax.broadcasted_iota(jnp.int32, (2 * tq, tq), 0)
        row = jnp.where(row >= tq, row - tq, row)
        col = lax.broadcasted_iota(jnp.int32, (2 * tq, tq), 1)
        block(col <= row)
        lam = _lambda_diff(lam_ref, lambda_init)
        for h in range(n_heads):
            o = acc_sc[h] / _softmax_total(l_sc, h)
            o = o[:tq] - lam * o[tq:]
            o = _rms(o) * sub_ref[...] * (1.0 - lambda_init)
            o_ref[:, h * LANES:(h + 1) * LANES] = o.astype(o_ref.dtype)


def _diff_meta_kernel(q_ref, km_ref, vm_ref, lam_ref, sub_ref, prev_ref, o_ref,
                      *, n_heads, qk, n_meta, lambda_init):
    del prev_ref
    row = lax.broadcasted_iota(jnp.int32, (2 * MB, MB), 0)
    row = jnp.where(row >= MB, row - MB, row)
    col = lax.broadcasted_iota(jnp.int32, (2 * MB, MB), 1)
    visible = (col <= row) & (col < n_meta)
    lam = _lambda_diff(lam_ref, lambda_init)
    for h in range(n_heads):
        s = jnp.where(visible, _nt(_diff_split(q_ref[h], qk), km_ref[h]), NEG_INF)
        p = jnp.exp2(s - jnp.max(s, axis=1, keepdims=True))
        o = _nn(p.astype(BF16), vm_ref[h]) / jnp.sum(p, axis=1, keepdims=True)
        o = o[:MB] - lam * o[MB:]
        o = _rms(o) * sub_ref[...] * (1.0 - lambda_init)
        o_ref[:, h * LANES:(h + 1) * LANES] = o.astype(o_ref.dtype)


def _diff_prompt(q, k, v, lam, subln, *, t_real, n_meta, tq, qk, lambda_init):
    b, n_heads, tp, _ = q.shape
    nq = t_real // tq
    qi, kj = _tri_schedule(nq)
    meta_blk = t_real // MB
    qmap = lambda bb, s, qi_r, kj_r: (bb, 0, qi_r[s], 0)
    kmap = lambda bb, s, qi_r, kj_r: (bb, 0, kj_r[s], 0)
    mmap = lambda bb, s, qi_r, kj_r: (bb, 0, meta_blk, 0)
    cmap = lambda bb, s, qi_r, kj_r: (0, 0)
    grid_spec = pltpu.PrefetchScalarGridSpec(
        num_scalar_prefetch=2,
        grid=(b, int(qi.shape[0])),
        in_specs=[pl.BlockSpec((None, n_heads, tq, LANES), qmap),
                  pl.BlockSpec((None, n_heads, tq, LANES), kmap),
                  pl.BlockSpec((None, n_heads, tq, LANES), kmap),
                  pl.BlockSpec((None, n_heads, MB, LANES), mmap),
                  pl.BlockSpec((None, n_heads, MB, LANES), mmap),
                  pl.BlockSpec(lam.shape, cmap),
                  pl.BlockSpec(subln.shape, cmap)],
        out_specs=pl.BlockSpec((None, tq, n_heads * LANES), lambda bb, s, qi_r, kj_r: (bb, qi_r[s], 0)),
        scratch_shapes=[pltpu.VMEM((n_heads, 2 * tq, LANES), F32), pltpu.VMEM((n_heads, 2 * tq, LANES), F32),
                        pltpu.VMEM((n_heads, 2 * tq, LANES), F32)],
    )
    nbytes = (6 * n_heads * tq * LANES * 2 + 2 * tq * n_heads * LANES * 2
              + n_heads * 2 * tq * 3 * LANES * 4 + 8 * 2 * tq * tq * 4 + (8 << 20))
    o_real = pl.pallas_call(
        functools.partial(_diff_prompt_kernel, n_heads=n_heads, qk=qk, n_meta=n_meta, tq=tq,
                          lambda_init=lambda_init),
        out_shape=jax.ShapeDtypeStruct((b, tp, n_heads * LANES), BF16),
        grid_spec=grid_spec,
        compiler_params=_params(("parallel", "arbitrary"), nbytes),
        name="diff_prompt_attention",
    )(qi, kj, q, k, v, k, v, lam, subln)
    mmap1 = lambda bb: (bb, 0, meta_blk, 0)
    return pl.pallas_call(
        functools.partial(_diff_meta_kernel, n_heads=n_heads, qk=qk, n_meta=n_meta, lambda_init=lambda_init),
        out_shape=jax.ShapeDtypeStruct((b, tp, n_heads * LANES), BF16),
        grid=(b,),
        in_specs=[pl.BlockSpec((None, n_heads, MB, LANES), mmap1),
                  pl.BlockSpec((None, n_heads, MB, LANES), mmap1),
                  pl.BlockSpec((None, n_heads, MB, LANES), mmap1),
                  pl.BlockSpec(lam.shape, lambda bb: (0, 0)),
                  pl.BlockSpec(subln.shape, lambda bb: (0, 0)),
                  pl.BlockSpec(memory_space=pl.ANY)],
        out_specs=pl.BlockSpec((None, MB, n_heads * LANES), lambda bb: (bb, meta_blk, 0)),
        input_output_aliases={5: 0},
        compiler_params=_params(("parallel",), 16 << 20),
        name="diff_meta_attention",
    )(q, k, v, lam, subln, o_real)


def _new_token_mask(rows, page, t_new):
    t = lax.broadcasted_iota(jnp.int32, (rows, page), 0) % t_new
    col = lax.broadcasted_iota(jnp.int32, (rows, page), 1)
    return col <= t


def _mla_decode_kernel(pt_ref, q_ref, cn_ref, krn_ref, *rest, n_pages, kl, rope, t_new):
    del pt_ref
    lat_refs = rest[:n_pages]
    rope_refs = rest[n_pages:2 * n_pages]
    o_ref, m_sc, l_sc, acc_sc = rest[2 * n_pages:]
    j = pl.program_id(1)
    q_lat = q_ref[:, :kl]
    q_pe = q_ref[:, kl:kl + rope]
    rows = q_ref.shape[0]
    page = cn_ref.shape[0]

    @pl.when(j == 0)
    def _init():
        cn = cn_ref[...].astype(BF16)
        s = _nt(q_lat, cn) + _nt(q_pe, krn_ref[...].astype(BF16))
        p = _softmax_first(jnp.where(_new_token_mask(rows, page, t_new), s, NEG_INF), m_sc, l_sc, 0)
        acc_sc[...] = _nn(p.astype(BF16), cn)

    lat = jnp.concatenate([r[...].astype(BF16) for r in lat_refs], axis=0)
    kr_t = jnp.concatenate([r[...].astype(BF16) for r in rope_refs], axis=1)
    alpha, p = _softmax_next(_nt(q_lat, lat) + _nn(q_pe, kr_t), m_sc, l_sc, 0)
    acc_sc[...] = _lane_tile(alpha, kl) * acc_sc[...] + _nn(p.astype(BF16), lat)

    @pl.when(j == pl.num_programs(1) - 1)
    def _finish():
        o_ref[...] = (acc_sc[...] / _softmax_total(l_sc, 0)).astype(o_ref.dtype)


def _page_specs(pool, li, n_pages, n_tab):
    blk = (None, None) + pool.shape[2:]
    specs = []
    for g in range(n_pages):
        specs.append(pl.BlockSpec(
            blk, lambda b, j, pt, g=g: (li, pt[b * n_tab + j * n_pages + g]) + (0,) * (pool.ndim - 2)))
    return specs


def _mla_decode(q, c_new, kr_new, lat_pool, rope_pool_t, page_table, li, *, t_new, pages_per_step):
    n_seq, rows, kw = q.shape
    kl = kw - LANES
    rope, page = rope_pool_t.shape[2], rope_pool_t.shape[3]
    n_tab = page_table.shape[1]
    g = pages_per_step
    seq3 = lambda b, j, pt: (b, 0, 0)
    grid_spec = pltpu.PrefetchScalarGridSpec(
        num_scalar_prefetch=1,
        grid=(n_seq, n_tab // g),
        in_specs=[pl.BlockSpec((None, rows, kw), seq3),
                  pl.BlockSpec((None, page, kl), seq3),
                  pl.BlockSpec((None, page, rope), seq3)]
                 + _page_specs(lat_pool, li, g, n_tab) + _page_specs(rope_pool_t, li, g, n_tab),
        out_specs=pl.BlockSpec((None, rows, kl), seq3),
        scratch_shapes=[pltpu.VMEM((1, rows, LANES), F32), pltpu.VMEM((1, rows, LANES), F32),
                        pltpu.VMEM((rows, kl), F32)],
    )
    nbytes = 2 * g * page * (kl + rope) * 4 + 3 * g * page * (kl + rope) * 2 + (8 << 20)
    return pl.pallas_call(
        functools.partial(_mla_decode_kernel, n_pages=g, kl=kl, rope=rope, t_new=t_new),
        out_shape=jax.ShapeDtypeStruct((n_seq, rows, kl), BF16),
        grid_spec=grid_spec,
        compiler_params=_params(("parallel", "arbitrary"), nbytes),
        name="mla_paged_attention",
    )(page_table.reshape(-1), q, c_new, kr_new, *([lat_pool] * g), *([rope_pool_t] * g))


def _diff_decode_kernel(pt_ref, q_ref, kn_ref, vn_ref, lam_ref, sub_ref, *rest,
                        n_pages, n_heads, t_new, lambda_init):
    del pt_ref
    k_refs = rest[:n_pages]
    v_refs = rest[n_pages:2 * n_pages]
    o_ref, m_sc, l_sc, acc_sc = rest[2 * n_pages:]
    j = pl.program_id(1)
    q = q_ref[...]
    rows, width = q.shape
    page = kn_ref.shape[0]

    @pl.when(j == 0)
    def _init():
        s = _nt(q, kn_ref[...].astype(BF16))
        p = _softmax_first(jnp.where(_new_token_mask(rows, page, t_new), s, NEG_INF), m_sc, l_sc, 0)
        acc_sc[...] = _nn(p.astype(BF16), vn_ref[...].astype(BF16))

    k_t = jnp.concatenate([r[...].astype(BF16) for r in k_refs], axis=1)
    alpha, p = _softmax_next(_nn(q, k_t), m_sc, l_sc, 0)
    v = jnp.concatenate(
        [jnp.concatenate([r[pl.ds(h, page, stride=n_heads), :].astype(BF16) for h in range(n_heads)], axis=1)
         for r in v_refs], axis=0)
    acc_sc[...] = _lane_tile(alpha, width) * acc_sc[...] + _nn(p.astype(BF16), v)

    @pl.when(j == pl.num_programs(1) - 1)
    def _finish():
        lam = _lambda_diff(lam_ref, lambda_init)
        o = acc_sc[...] / _softmax_total(l_sc, 0)
        per_head = rows // n_heads
        for h in range(n_heads):
            blk = o[h * per_head:(h + 1) * per_head, h * LANES:(h + 1) * LANES]
            d = blk - lam * pltpu.roll(blk, per_head // 2, 0)
            d = _rms(d) * sub_ref[...] * (1.0 - lambda_init)
            o_ref[h * per_head:(h + 1) * per_head, :] = d.astype(o_ref.dtype)


def _diff_decode(q, k_new, v_new, lam, subln, k_pool_t, v_pool, page_table, li,
                 *, t_new, n_heads, pages_per_step, lambda_init):
    n_seq, rows, width = q.shape
    page = k_pool_t.shape[3]
    n_tab = page_table.shape[1]
    g = pages_per_step
    seq3 = lambda b, j, pt: (b, 0, 0)
    cst = lambda b, j, pt: (0, 0)
    grid_spec = pltpu.PrefetchScalarGridSpec(
        num_scalar_prefetch=1,
        grid=(n_seq, n_tab // g),
        in_specs=[pl.BlockSpec((None, rows, width), seq3),
                  pl.BlockSpec((None, page, width), seq3),
                  pl.BlockSpec((None, page, width), seq3),
                  pl.BlockSpec(lam.shape, cst),
                  pl.BlockSpec(subln.shape, cst)]
                 + _page_specs(k_pool_t, li, g, n_tab) + _page_specs(v_pool, li, g, n_tab),
        out_specs=pl.BlockSpec((None, rows, LANES), seq3),
        scratch_shapes=[pltpu.VMEM((1, rows, LANES), F32), pltpu.VMEM((1, rows, LANES), F32),
                        pltpu.VMEM((rows, width), F32)],
    )
    nbytes = 2 * 2 * g * page * width * 4 + 3 * g * page * width * 2 + (8 << 20)
    return pl.pallas_call(
        functools.partial(_diff_decode_kernel, n_pages=g, n_heads=n_heads, t_new=t_new, lambda_init=lambda_init),
        out_shape=jax.ShapeDtypeStruct((n_seq, rows, LANES), F32),
        grid_spec=grid_spec,
        compiler_params=_params(("parallel", "arbitrary"), nbytes),
        name="diff_paged_attention",
    )(page_table.reshape(-1), q, k_new, v_new, lam, subln, *([k_pool_t] * g), *([v_pool] * g))


def _post_kernel(*refs, n_act, alpha, tf):
    x_ref = refs[0]
    act_refs = refs[1:1 + n_act]
    w_refs = refs[1 + n_act:1 + 2 * n_act]
    g1_ref, b1_ref, g2_ref, b2_ref, w1_ref, w2_ref, o_ref = refs[1 + 2 * n_act:]
    mix = _nn(act_refs[0][...], w_refs[0][...])
    for a_ref, w_ref in zip(act_refs[1:], w_refs[1:]):
        mix = mix + _nn(a_ref[...], w_ref[...])
    x1 = _layer_norm(alpha * x_ref[...] + mix, g1_ref[...], b1_ref[...])
    x1b = x1.astype(BF16)
    d_ff = w1_ref.shape[1]
    acc = None
    for c in range(d_ff // tf):
        hid = jnp.maximum(_nn(x1b, w1_ref[:, c * tf:(c + 1) * tf]), 0.0)
        part = _nn((hid * hid).astype(BF16), w2_ref[c * tf:(c + 1) * tf, :])
        acc = part if acc is None else acc + part
    o_ref[...] = _layer_norm(alpha * x1 + acc, g2_ref[...], b2_ref[...])


def _post(x, acts, ws, g1, b1, g2, b2, w1, w2, *, tm, alpha, t_out=None):
    bx, tx, d = x.shape
    t_out = tx if t_out is None else t_out
    row = lambda b, i: (b, i, 0)
    tf = min(512, w1.shape[1])
    nbytes = (4 * tm * d * 4 + sum(2 * tm * a.shape[2] * 2 for a in acts) + sum(w.size * 2 for w in ws)
              + (w1.size + w2.size) * 2 + 6 * tm * d * 4 + 4 * tm * tf * 4 + (8 << 20))
    return pl.pallas_call(
        functools.partial(_post_kernel, n_act=len(acts), alpha=alpha, tf=tf),
        out_shape=jax.ShapeDtypeStruct((bx, t_out, d), F32),
        grid=(bx, t_out // tm),
        in_specs=[pl.BlockSpec((None, tm, d), row)]
                 + [pl.BlockSpec((None, tm, a.shape[2]), row) for a in acts]
                 + [_const_spec(w) for w in ws]
                 + [_const_spec(v) for v in (g1, b1, g2, b2, w1, w2)],
        out_specs=pl.BlockSpec((None, tm, d), row),
        compiler_params=_params(("parallel", "parallel"), nbytes),
        name="out_proj_mlp",
    )(x, *acts, *ws, g1, b1, g2, b2, w1, w2)


def _log_sigmoid(x):
    return -(jnp.maximum(-x, 0.0) + jnp.log1p(jnp.exp(-jnp.abs(x))))


def _cd_in_kernel(x_ref, cos_ref, sin_ref, w_ref, gb_ref, nrm_ref,
                  rq_ref, rk_ref, rv_ref, rg_ref, mq_ref, mk_ref, mv_ref, mg_ref, gt_ref,
                  *, qw, vw, n_ml, rk_scale, mk_scale):
    xb = x_ref[...].astype(BF16)
    cos = cos_ref[...]
    sin = sin_ref[...]

    def mm(lo, width):
        return _nn(xb, w_ref[:, lo:lo + width])

    rq_ref[...] = mm(0, qw) * cos + mm(qw, qw) * sin
    rk_ref[...] = (mm(2 * qw, qw) * cos + mm(3 * qw, qw) * sin) * rk_scale
    o = 4 * qw
    rv_ref[...] = mm(o, vw).astype(BF16)
    rg = mm(o + vw, vw)
    rg_ref[...] = rg * jax.nn.sigmoid(rg)
    o = o + 2 * vw
    mq_ref[...] = mm(o, qw)
    mk_ref[...] = mm(o + qw, qw) * mk_scale
    o = o + 2 * qw
    mv_ref[...] = mm(o, vw).astype(BF16)
    mg_ref[...] = jax.nn.sigmoid(mm(o + vw, vw)) * nrm_ref[...]
    pre = mm(o + 2 * vw, LANES) + gb_ref[...]
    lane = lax.broadcasted_iota(jnp.int32, pre.shape, 1)
    gt_ref[...] = jnp.where(lane < n_ml, pre, _log_sigmoid(pre))


def _cd_in(x, cos, sin, w_aug, gate_bias, ml_norm, *, tm, dims):
    bx, tx, d = x.shape
    qw, vw = dims["cd_qw"], dims["cd_vw"]
    row = lambda b, i: (b, i, 0)
    tab = lambda b, i: (i, 0)
    shapes = [(qw, F32), (qw, F32), (vw, BF16), (vw, F32), (qw, F32), (qw, F32), (vw, BF16), (vw, F32), (LANES, F32)]
    nbytes = (2 * tm * d * 4 + w_aug.size * 2 + 2 * tm * sum(w * (4 if t == F32 else 2) for w, t in shapes)
              + 8 * tm * vw * 4 + (8 << 20))
    return pl.pallas_call(
        functools.partial(_cd_in_kernel, qw=qw, vw=vw, n_ml=dims["n_ml"],
                          rk_scale=dims["rk_scale"], mk_scale=dims["mk_scale"]),
        out_shape=tuple(jax.ShapeDtypeStruct((bx, tx, w), t) for w, t in shapes),
        grid=(bx, tx // tm),
        in_specs=[pl.BlockSpec((None, tm, d), row),
                  pl.BlockSpec((tm, qw), tab), pl.BlockSpec((tm, qw), tab),
                  _const_spec(w_aug), _const_spec(gate_bias), _const_spec(ml_norm)],
        out_specs=tuple(pl.BlockSpec((None, tm, w), row) for w, _ in shapes),
        compiler_params=_params(("parallel", "parallel"), nbytes),
        name="cd_in_proj",
    )(x, cos, sin, w_aug, gate_bias, ml_norm)


def _split_bf16(x):
    hi = x.astype(BF16)
    lo = (x - hi.astype(F32)).astype(BF16)
    return hi, lo


def _chunk_kernel(ids_ref, rq_ref, rk_ref, rv_ref, rg_ref, mq_ref, mk_ref, mv_ref, mg_ref, gt_ref,
                  dec_ref, lgr_ref, lgc_ref, s0_ref, c0_ref, n0_ref, m0_ref,
                  o_ref, s_out, c_out, n_out, m_out,
                  s_sc, c_sc, n_sc, m_sc,
                  *, n_heads, dk, dv, first_valid):
    del ids_ref
    ci = pl.program_id(1)
    n_valid = jnp.where(ci == 0, first_valid, MB)
    qw = n_heads * dk

    @pl.when(ci == 0)
    def _load_state():
        s_sc[...] = s0_ref[...]
        c_sc[...] = c0_ref[...]
        n_sc[...] = n0_ref[...]
        m_sc[...] = m0_ref[...]

    idx_c = lax.broadcasted_iota(jnp.int32, (MB, 1), 0)
    live_c = idx_c < n_valid
    lane_q = lax.broadcasted_iota(jnp.int32, (MB, qw), 1)
    t_f = idx_c.astype(F32)
    nv_f = n_valid.astype(F32)

    lgr = lgr_ref[...]
    rq = rq_ref[...]
    rk = jnp.where(live_c, rk_ref[...], 0.0)
    rv = rv_ref[...]
    q_dec = rq * jnp.exp(lgr * (t_f + 1.0))
    k_dec = (rk * jnp.exp(lgr * jnp.where(live_c, nv_f - 1.0 - t_f, 0.0))).astype(BF16)
    rk_b = rk.astype(BF16)
    s_all = s_sc[...]
    s_b = s_all.astype(BF16)
    upd = _tn(k_dec, rv)
    s_dec = jnp.exp(lgc_ref[...] * nv_f)
    for h in range(n_heads):
        head = (lane_q >= h * dk) & (lane_q < (h + 1) * dk)
        inner = _nt(jnp.where(head, rq, 0.0).astype(BF16), rk_b) * dec_ref[h]
        o = _nn(inner.astype(BF16), rv[:, h * dv:(h + 1) * dv]) + _nn(jnp.where(head, q_dec, 0.0).astype(BF16), s_b)
        o = _head_norm(o) * rg_ref[:, h * dv:(h + 1) * dv]
        o_ref[:, h * dv:(h + 1) * dv] = o.astype(o_ref.dtype)
        s_sc[h * dk:(h + 1) * dk, :] = (s_dec[h * dk:(h + 1) * dk] * s_all[h * dk:(h + 1) * dk]
                                        + upd[h * dk:(h + 1) * dk, h * dv:(h + 1) * dv])

    gt = gt_ref[...]
    lane_g = lax.broadcasted_iota(jnp.int32, gt.shape, 1)
    gt = jnp.where(live_c, gt, jnp.where(lane_g < n_heads, NEG_INF, 0.0))
    row_i = lax.broadcasted_iota(jnp.int32, (MB, MB), 0)
    col_i = lax.broadcasted_iota(jnp.int32, (MB, MB), 1)
    causal = col_i <= row_i
    tri = causal.astype(BF16)
    g_hi, g_lo = _split_bf16(gt)
    cum_c = _nn(tri, g_hi) + _nn(tri, g_lo)
    gt_r = gt.T
    cum_r = cum_c.T
    mq = mq_ref[...]
    mk = jnp.where(live_c, mk_ref[...], 0.0)
    mk_b = mk.astype(BF16)
    mv = mv_ref[...]
    c_cat = c_sc[...]
    c_b = c_cat.astype(BF16)
    n_cat = n_sc[...]
    m_all = m_sc[...]
    c_new = jnp.zeros_like(c_cat)
    n_new = jnp.zeros_like(n_cat)
    g_keep = jnp.zeros_like(n_cat)
    m_new = jnp.zeros_like(m_all)
    lane_m = lax.broadcasted_iota(jnp.int32, m_all.shape, 1)
    lane_n = lax.broadcasted_iota(jnp.int32, n_cat.shape, 1)
    for h in range(n_heads):
        head = (lane_q >= h * dk) & (lane_q < (h + 1) * dk)
        head_n = (lane_n >= h * dk) & (lane_n < (h + 1) * dk)
        b_c = cum_c[:, n_heads + h:n_heads + h + 1]
        i_c = gt[:, h:h + 1]
        b_r = cum_r[n_heads + h:n_heads + h + 1, :]
        i_r = gt_r[h:h + 1, :]
        m_prev = m_all[:, h:h + 1]
        d = jnp.where(causal, b_c - b_r + i_r, NEG_INF)
        g = b_c + m_prev
        m_t = jnp.maximum(g, jnp.max(d, axis=1, keepdims=True))
        w = jnp.exp(d - m_t)
        gs = jnp.exp(g - m_t)
        qh = jnp.where(head, mq, 0.0)
        qh_b = qh.astype(BF16)
        a = _nt(qh_b, mk_b) * w
        num = _nn(a.astype(BF16), mv[:, h * dv:(h + 1) * dv]) + gs * _nt(qh_b, c_b)
        den = jnp.sum(a, axis=1, keepdims=True) + gs * jnp.sum(qh * n_cat, axis=1, keepdims=True)
        hid = num / jnp.maximum(jnp.abs(den), jnp.exp(-m_t))
        hid = _head_norm(hid) * mg_ref[:, h * dv:(h + 1) * dv]
        o_ref[:, n_heads * dv + h * dv:n_heads * dv + (h + 1) * dv] = hid.astype(o_ref.dtype)
        m_last = m_t[MB - 1:MB]
        w_last = jnp.exp(b_c[MB - 1:MB] - b_c + i_c - m_last)
        g_last = gs[MB - 1:MB]
        kh = jnp.where(head, mk, 0.0)
        c_new = c_new + _tn((mv[:, h * dv:(h + 1) * dv].astype(F32) * w_last).astype(BF16), kh.astype(BF16))
        n_new = n_new + jnp.sum(kh * w_last, axis=0, keepdims=True)
        g_keep = jnp.where(head_n, g_last, g_keep)
        m_new = jnp.where(lane_m == h, m_last, m_new)
    c_sc[...] = g_keep * c_cat + c_new
    n_sc[...] = g_keep * n_cat + n_new
    m_sc[...] = m_new

    @pl.when(ci == pl.num_programs(1) - 1)
    def _store_state():
        s_out[...] = s_sc[...]
        c_out[...] = c_sc[...]
        n_out[...] = n_sc[...]
        m_out[...] = m_sc[...]


def _chunk_scan(rq, rk, rv, rg, mq, mk, mv, mg, gt, dec, lgr, lgc, s0, c0, n0, m0,
                *, n_heads, dk, dv, blocks, first_valid):
    bx, tx, qw = rq.shape
    vw = rv.shape[2]
    blk_ids = jnp.asarray(blocks, jnp.int32)
    row = lambda b, c, ids: (b, ids[c], 0)
    st = lambda b, c, ids: (b, 0, 0)
    cst3 = lambda b, c, ids: (0, 0, 0)
    cst2 = lambda b, c, ids: (0, 0)
    state_shapes = [(qw, dv), (dv, qw), (1, qw), (1, LANES)]
    grid_spec = pltpu.PrefetchScalarGridSpec(
        num_scalar_prefetch=1,
        grid=(bx, len(blocks)),
        in_specs=[pl.BlockSpec((None, MB, qw), row), pl.BlockSpec((None, MB, qw), row),
                  pl.BlockSpec((None, MB, vw), row), pl.BlockSpec((None, MB, vw), row),
                  pl.BlockSpec((None, MB, qw), row), pl.BlockSpec((None, MB, qw), row),
                  pl.BlockSpec((None, MB, vw), row), pl.BlockSpec((None, MB, vw), row),
                  pl.BlockSpec((None, MB, LANES), row),
                  pl.BlockSpec(dec.shape, cst3), pl.BlockSpec(lgr.shape, cst2), pl.BlockSpec(lgc.shape, cst2)]
                 + [pl.BlockSpec((None,) + s, st) for s in state_shapes],
        out_specs=[pl.BlockSpec((None, MB, 2 * vw), row)] + [pl.BlockSpec((None,) + s, st) for s in state_shapes],
        scratch_shapes=[pltpu.VMEM(s, F32) for s in state_shapes],
    )
    out_shape = [jax.ShapeDtypeStruct((bx, tx, 2 * vw), BF16)] + [jax.ShapeDtypeStruct((bx,) + s, F32) for s in state_shapes]
    return pl.pallas_call(
        functools.partial(_chunk_kernel, n_heads=n_heads, dk=dk, dv=dv, first_valid=first_valid),
        out_shape=out_shape,
        grid_spec=grid_spec,
        compiler_params=_params(("parallel", "arbitrary"), 32 << 20),
        name="retention_mlstm_chunks",
    )(blk_ids, rq, rk, rv, rg, mq, mk, mv, mg, gt, dec, lgr, lgc, s0, c0, n0, m0)


def _rope_tables(pos, half, reps, width):
    freqs = ROPE_BASE ** (-jnp.arange(half, dtype=F32) / half)
    ang = pos.astype(F32)[:, None] * freqs[None, :]
    cos, sin = jnp.cos(ang), jnp.sin(ang)
    cos = jnp.tile(jnp.concatenate([cos, cos], axis=1), (1, reps))
    sin = jnp.tile(jnp.concatenate([-sin, sin], axis=1), (1, reps))
    pad = width - cos.shape[1]
    return jnp.pad(cos, ((0, 0), (0, pad))), jnp.pad(sin, ((0, 0), (0, pad)))


def _swap_halves(w, group):
    shp = w.shape
    w = w.reshape(shp[:-1] + (shp[-1] // group, 2, group // 2))
    return w[..., ::-1, :].reshape(shp)


def _pad_cols(w, width):
    return jnp.pad(w, ((0, 0), (0, width - w.shape[1])))


def _pad_page(rows, n_seq, t_new, page):
    w = rows.shape[-1]
    return jnp.pad(rows.reshape(n_seq, t_new, w), ((0, 0), (0, page - t_new), (0, 0)))


def _meta_first(a, t_real, n_meta):
    return jnp.concatenate([a[:, t_real:t_real + n_meta], a[:, :t_real]], axis=1)


def kernel(x_prompt, x_sample, cache_mla_latent, cache_mla_rope, cache_diff_k, cache_diff_v, state_ret, state_mlstm_C, state_mlstm_n, state_mlstm_m, page_table, meta_tokens, ab_w_in, mla_q_norm, mla_w_uq, mla_kv_norm, mla_w_uk, mla_w_uv, diff_lambda_q1, diff_lambda_k1, diff_lambda_q2, diff_lambda_k2, diff_subln, ab_w_out, cd_w_in, ml_b_i, ml_b_f, ml_norm, cd_w_out, ln1_g, ln1_b, ln2_g, ln2_b, mlp_w1, mlp_w2):
    b, t_real, d = x_prompt.shape
    n_seq, t_new, _ = x_sample.shape
    n_meta = meta_tokens.shape[0]
    depth = ln1_g.shape[0]
    alpha = (2 * depth) ** 0.25
    page = cache_mla_latent.shape[2]
    past_len = page_table.shape[1] * page
    ql, kl, rope = mla_q_norm.shape[1], mla_kv_norm.shape[1], cache_mla_rope.shape[3]
    n_mla, nope, mla_v = mla_w_uk.shape[2], mla_w_uk.shape[3], mla_w_uv.shape[3]
    n_diff, diff_qk, diff_v = cache_diff_k.shape[3], cache_diff_k.shape[5], cache_diff_v.shape[4]
    n_ret, ret_dk, ret_dv = state_ret.shape[2], state_ret.shape[3], state_ret.shape[4]
    n_ml, ml_dv, ml_dk = state_mlstm_C.shape[2], state_mlstm_C.shape[3], state_mlstm_C.shape[4]
    assert 2 * diff_qk == LANES and diff_v == LANES and n_meta <= MB and t_real % MB == 0 and page == MB
    assert (n_ret, ret_dk, ret_dv) == (n_ml, ml_dk, ml_dv) and ret_dv == LANES and t_new <= MB
    dw = n_diff * LANES
    dims = dict(ql=ql, kl=kl, dw=dw, rope=rope, n_mla=n_mla, n_diff=n_diff, dq_scale=diff_qk ** -0.5 * LOG2E,
                cd_qw=n_ret * ret_dk, cd_vw=n_ret * ret_dv, n_ml=n_ml, rk_scale=ret_dk ** -0.5, mk_scale=ml_dk ** -0.5)
    tp = t_real + MB
    n_rows_s = n_seq * t_new
    tm_p = _tile(tp, 640)
    tm_s = _tile(n_rows_s, 512)
    tq = _tile(t_real, 512)

    xp = jnp.concatenate([x_prompt, jnp.broadcast_to(meta_tokens.astype(x_prompt.dtype)[None], (b, n_meta, d)),
                          jnp.zeros((b, MB - n_meta, d), x_prompt.dtype)], axis=1)
    xs = x_sample.reshape(1, n_rows_s, d)
    pos_p = jnp.concatenate([n_meta + jnp.arange(t_real), jnp.arange(MB)])
    pos_s = past_len + (jnp.arange(n_rows_s) % t_new)

    prompt_out, sample_out = {}, {}
    for layer in range(depth):
        g1, b1, g2, b2 = (v[layer][None] for v in (ln1_g, ln1_b, ln2_g, ln2_b))
        w1, w2 = mlp_w1[layer].astype(BF16), mlp_w2[layer].astype(BF16)
        last = layer == depth - 1
        if layer % 2 == 0:
            li = layer // 2
            lambda_init = 0.8 - 0.6 * math.exp(-0.3 * layer)
            scale = (nope + rope) ** -0.5 * LOG2E
            w_in = ab_w_in[li]
            o = [0, ql, ql + kl, ql + kl + rope, ql + kl + rope + dw, ql + kl + rope + 2 * dw, ql + kl + rope + 3 * dw]
            w_cq, w_ckv, w_kr, w_dq, w_dk, w_dv = (w_in[:, o[i]:o[i + 1]] for i in range(6))
            w_aug = jnp.concatenate([w_cq, w_ckv, w_dq, w_dk, w_dv, _pad_cols(w_kr, LANES),
                                     _pad_cols(_swap_halves(w_kr, rope), LANES)], axis=1).astype(BF16)
            w_uq = mla_w_uq[li].reshape(ql, n_mla, nope + rope)
            w_abs = _wprod(w_uq[:, :, :nope].transpose(1, 0, 2), mla_w_uk[li].transpose(1, 0, 2), True)
            w_pe = w_uq[:, :, nope:].transpose(1, 0, 2)
            pe_pad = ((0, 0), (0, 0), (0, LANES - rope))
            wq_big = (jnp.concatenate([w_abs, jnp.pad(w_pe, pe_pad), jnp.pad(_swap_halves(w_pe, rope), pe_pad)], axis=2)
                      * scale).transpose(1, 0, 2).reshape(ql, n_mla * (kl + 2 * LANES)).astype(BF16)
            w_out = ab_w_out[li]
            w_comb = _wprod(mla_w_uv[li].transpose(1, 0, 2), w_out[:n_mla * mla_v].reshape(n_mla, mla_v, d), False)
            w_comb = w_comb.reshape(n_mla * kl, d).astype(BF16)
            w_out_diff = w_out[n_mla * mla_v:].astype(BF16)
            qn, kvn = mla_q_norm[li][None], mla_kv_norm[li][None]
            lam = jnp.stack([diff_lambda_q1[li], diff_lambda_k1[li], diff_lambda_q2[li], diff_lambda_k2[li]])
            subln = diff_subln[li][None]

            cos_p, sin_p = _rope_tables(pos_p, rope // 2, 1, LANES)
            cos_s, sin_s = _rope_tables(pos_s, rope // 2, 1, LANES)
            c_p, kr_p, kcat_p, q_p, dq_p, dk_p, dv_p, dkb_p, dvb_p = _ab_in(
                xp, cos_p, sin_p, w_aug, wq_big, qn, kvn, tm=tm_p, dims=dims)
            c_s, kr_s, _, q_s, dq_s, dk_s, dv_s, _, _ = _ab_in(
                xs, cos_s, sin_s, w_aug, wq_big, qn, kvn, tm=tm_s, dims=dims)

            o_lat_p = _mla_prompt(q_p, kcat_p, t_real=t_real, n_meta=n_meta, tq=tq)
            o_diff_p = _diff_prompt(dq_p, dkb_p, dvb_p, lam, subln, t_real=t_real, n_meta=n_meta, tq=tq,
                                    qk=diff_qk, lambda_init=lambda_init)

            kw = kl + LANES
            q_dec = q_s.reshape(n_mla, n_seq, t_new, kw).transpose(1, 0, 2, 3).reshape(n_seq, n_mla * t_new, kw)
            o_lat_s = _mla_decode(q_dec, _pad_page(c_s[0], n_seq, t_new, page), _pad_page(kr_s[0], n_seq, t_new, page),
                                  cache_mla_latent, cache_mla_rope.transpose(0, 1, 3, 2), page_table, li,
                                  t_new=t_new, pages_per_step=_tile_pages(page_table.shape[1], 32))
            o_lat_s = (o_lat_s.reshape(n_seq, n_mla, t_new, kl).transpose(0, 2, 1, 3)
                       .reshape(1, n_rows_s, n_mla * kl))
            dq5 = dq_s.reshape(n_diff, n_seq, t_new, 2, diff_qk).transpose(1, 0, 3, 2, 4)
            eye_h = jnp.eye(n_diff, dtype=BF16)
            eye_m = jnp.eye(2, dtype=BF16)
            q_bd = (dq5[:, :, :, :, None, None, :] * eye_h[None, :, None, None, :, None, None]
                    * eye_m[None, None, :, None, None, :, None]).reshape(n_seq, n_diff * 2 * t_new, dw)
            k_pool_t = cache_diff_k.transpose(0, 1, 3, 4, 5, 2).reshape(cache_diff_k.shape[:2] + (dw, page))
            v_pool = cache_diff_v.reshape(cache_diff_v.shape[:2] + (page * n_diff, diff_v))
            o_diff_s = _diff_decode(q_bd, _pad_page(dk_s[0], n_seq, t_new, page), _pad_page(dv_s[0], n_seq, t_new, page),
                                    lam, subln, k_pool_t, v_pool, page_table, li, t_new=t_new, n_heads=n_diff,
                                    pages_per_step=_tile_pages(page_table.shape[1], 16), lambda_init=lambda_init)
            o_diff_s = (o_diff_s.reshape(n_seq, n_diff, 2, t_new, LANES)[:, :, 0].transpose(0, 2, 1, 3)
                        .reshape(1, n_rows_s, dw).astype(BF16))

            acts_p, acts_s, ws = [o_lat_p, o_diff_p], [o_lat_s, o_diff_s], [w_comb, w_out_diff]
            for name, rows_p, rows_s in (("c", c_p, c_s), ("kr", kr_p, kr_s), ("dk", dk_p, dk_s), ("dv", dv_p, dv_s)):
                prompt_out.setdefault(name, []).append(_meta_first(rows_p, t_real, n_meta))
                sample_out.setdefault(name, []).append(rows_s.reshape(n_seq, t_new, -1))
        else:
            lj = layer // 2
            qw, vw = dims["cd_qw"], dims["cd_vw"]
            w_in = cd_w_in[lj]
            widths = [qw, qw, vw, vw, qw, qw, vw, vw, n_ml, n_ml]
            offs = [0]
            for wdt in widths:
                offs.append(offs[-1] + wdt)
            w_rq, w_rk, w_rv, w_rg, w_mq, w_mk, w_mv, w_mo, w_mi, w_mf = (w_in[:, offs[i]:offs[i + 1]] for i in range(10))
            w_aug = jnp.concatenate([w_rq, _swap_halves(w_rq, ret_dk), w_rk, _swap_halves(w_rk, ret_dk), w_rv, w_rg,
                                     w_mq, w_mk, w_mv, w_mo,
                                     _pad_cols(jnp.concatenate([w_mi, w_mf], axis=1), LANES)], axis=1).astype(BF16)
            gate_bias = _pad_cols(jnp.concatenate([ml_b_i[lj], ml_b_f[lj]])[None], LANES)
            nrm = ml_norm[lj][None]
            w_out = cd_w_out[lj].astype(BF16)
            log_gamma = jnp.log1p(-jnp.power(2.0, -5.0 - jnp.arange(n_ret, dtype=F32)))
            idx = jnp.arange(MB, dtype=F32)
            diff_ts = idx[:, None] - idx[None, :]
            dec = jnp.where(diff_ts >= 0, jnp.exp(log_gamma[:, None, None] * jnp.maximum(diff_ts, 0.0)), 0.0)
            lgr = jnp.repeat(log_gamma, ret_dk)[None]
            lgc = jnp.repeat(log_gamma, ret_dk)[:, None]

            cos_p, sin_p = _rope_tables(pos_p, ret_dk // 2, n_ret, qw)
            cos_s, sin_s = _rope_tables(pos_s, ret_dk // 2, n_ret, qw)
            rows_p = _cd_in(xp, cos_p, sin_p, w_aug, gate_bias, nrm, tm=tm_p, dims=dims)
            rows_s = _cd_in(xs, cos_s, sin_s, w_aug, gate_bias, nrm, tm=tm_s, dims=dims)

            zeros = lambda *s: jnp.zeros(s, F32)
            chunk = functools.partial(_chunk_scan, n_heads=n_ret, dk=ret_dk, dv=ret_dv)
            o_p, s_p, c_p2, n_p, m_p = chunk(
                *rows_p, dec, lgr, lgc, zeros(b, qw, ret_dv), zeros(b, ret_dv, qw), zeros(b, 1, qw), zeros(b, 1, LANES),
                blocks=[t_real // MB] + list(range(t_real // MB)), first_valid=n_meta)
            rows_s_pad = [_pad_page(r[0], n_seq, t_new, MB) for r in rows_s]
            c0 = state_mlstm_C[lj].transpose(0, 2, 1, 3).reshape(n_seq, ml_dv, qw)
            o_s, s_s, c_s2, n_s, m_s = chunk(
                *rows_s_pad, dec, lgr, lgc, state_ret[lj].reshape(n_seq, qw, ret_dv), c0,
                state_mlstm_n[lj].reshape(n_seq, 1, qw), _pad_cols(state_mlstm_m[lj], LANES)[:, None],
                blocks=[0], first_valid=t_new)
            o_s = o_s[:, :t_new].reshape(1, n_rows_s, 2 * vw)

            acts_p, acts_s, ws = [o_p], [o_s], [w_out]
            for dst, s_, c_, n_, m_, nb in ((prompt_out, s_p, c_p2, n_p, m_p, b), (sample_out, s_s, c_s2, n_s, m_s, n_seq)):
                dst.setdefault("s", []).append(s_.reshape(nb, n_ret, ret_dk, ret_dv))
                dst.setdefault("C", []).append(c_.reshape(nb, ml_dv, n_ml, ml_dk).transpose(0, 2, 1, 3))
                dst.setdefault("n", []).append(n_.reshape(nb, n_ml, ml_dk))
                dst.setdefault("m", []).append(m_[:, 0, :n_ml])

        if last:
            xp = _post(xp, acts_p, ws, g1, b1, g2, b2, w1, w2, tm=_tile(t_real, 512), alpha=alpha, t_out=t_real)
        else:
            xp = _post(xp, acts_p, ws, g1, b1, g2, b2, w1, w2, tm=tm_p, alpha=alpha)
        xs = _post(xs, acts_s, ws, g1, b1, g2, b2, w1, w2, tm=tm_s, alpha=alpha)

    y_prompt = xp
    y_sample = xs.reshape(n_seq, t_new, d)
    k_shape_p = (b, n_meta + t_real, n_diff, 2, diff_qk)
    v_shape_p = (b, n_meta + t_real, n_diff, diff_v)
    k_shape_s = (n_seq, t_new, n_diff, 2, diff_qk)
    v_shape_s = (n_seq, t_new, n_diff, diff_v)
    return (y_prompt, y_sample,
            jnp.stack(prompt_out["c"]), jnp.stack(prompt_out["kr"]),
            jnp.stack([a.reshape(k_shape_p) for a in prompt_out["dk"]]),
            jnp.stack([a.reshape(v_shape_p) for a in prompt_out["dv"]]),
            jnp.stack(prompt_out["s"]), jnp.stack(prompt_out["C"]), jnp.stack(prompt_out["n"]), jnp.stack(prompt_out["m"]),
            jnp.stack(sample_out["c"]), jnp.stack(sample_out["kr"]),
            jnp.stack([a.reshape(k_shape_s) for a in sample_out["dk"]]),
            jnp.stack([a.reshape(v_shape_s) for a in sample_out["dv"]]),
            jnp.stack(sample_out["s"]), jnp.stack(sample_out["C"]), jnp.stack(sample_out["n"]), jnp.stack(sample_out["m"]))


def _tile_pages(n_pages, pref):
    g = min(pref, n_pages)
    while n_pages % g:
        g -= 1
    return g
```

```python
import functools
import math

import jax
import jax.numpy as jnp
from jax import lax
from jax.experimental import pallas as pl
from jax.experimental.pallas import tpu as pltpu

F32 = jnp.float32
BF16 = jnp.bfloat16

ROPE_BASE = 10000.0
NEG_INF = -1e30
EPS = 1e-5

LANES = 128
MB = 128
HEAD_UNROLL = 4
VMEM_CAP_BYTES = 56 << 20


def _vmem(nbytes):
    return int(min(VMEM_CAP_BYTES, max(16 << 20, nbytes)))


def _params(sem, nbytes):
    return pltpu.CompilerParams(dimension_semantics=sem, vmem_limit_bytes=_vmem(nbytes))


def _const_spec(arr):
    nd = arr.ndim
    return pl.BlockSpec(arr.shape, lambda *_: (0,) * nd, pipeline_mode=pl.Buffered(1))


def _tile(n, pref):
    if n <= pref:
        return n
    best = MB
    for t in range(MB, pref + 1, MB):
        if n % t == 0:
            best = t
    assert n % best == 0, (n, pref)
    return best


def _nt(a, b):
    return lax.dot_general(a, b, (((1,), (1,)), ((), ())), preferred_element_type=F32)


def _nn(a, b):
    return jnp.dot(a, b, preferred_element_type=F32)


def _tn(a, b):
    return _nn(a.T, b)


def _rms(x):
    return x * lax.rsqrt(jnp.mean(x * x, axis=-1, keepdims=True) + EPS)


def _layer_norm(x, g, b):
    mu = jnp.mean(x, axis=-1, keepdims=True)
    xc = x - mu
    var = jnp.mean(xc * xc, axis=-1, keepdims=True)
    return xc * lax.rsqrt(var + EPS) * g + b


def _head_norm(x):
    mu = jnp.mean(x, axis=-1, keepdims=True)
    xc = x - mu
    var = jnp.mean(xc * xc, axis=-1, keepdims=True)
    return xc * lax.rsqrt(var + EPS)


LOG2E = math.log2(math.e)


def _lane_tile(x, width):
    reps = width // LANES
    return x if reps == 1 else jnp.concatenate([x] * reps, axis=1)


def _lane_fold(p):
    part = p[:, :LANES]
    for i in range(1, p.shape[1] // LANES):
        part = part + p[:, i * LANES:(i + 1) * LANES]
    return part


def _softmax_first(s, m_ref, l_ref, idx):
    m = jnp.broadcast_to(jnp.max(s, axis=1, keepdims=True), (s.shape[0], LANES))
    p = jnp.exp2(s - _lane_tile(m, s.shape[1]))
    m_ref[idx] = m
    l_ref[idx] = _lane_fold(p)
    return p


def _softmax_next(s, m_ref, l_ref, idx):
    m_prev = m_ref[idx]
    m_new = jnp.maximum(m_prev, jnp.max(s, axis=1, keepdims=True))
    alpha = jnp.exp2(m_prev - m_new)
    p = jnp.exp2(s - _lane_tile(m_new, s.shape[1]))
    l_ref[idx] = alpha * l_ref[idx] + _lane_fold(p)
    m_ref[idx] = m_new
    return alpha, p


def _softmax_total(l_ref, idx):
    return jnp.sum(l_ref[idx], axis=1, keepdims=True)


def _wprod_kernel(a_ref, b_ref, o_ref, *, trans_b):
    a = a_ref[...].astype(BF16)
    b = b_ref[...].astype(BF16)
    o_ref[...] = _nt(a, b) if trans_b else _nn(a, b)


def _wprod(a, b, trans_b):
    h, m, _ = a.shape
    n = b.shape[1] if trans_b else b.shape[2]
    return pl.pallas_call(
        functools.partial(_wprod_kernel, trans_b=trans_b),
        out_shape=jax.ShapeDtypeStruct((h, m, n), F32),
        grid=(h,),
        in_specs=[pl.BlockSpec((None,) + a.shape[1:], lambda i: (i, 0, 0)),
                  pl.BlockSpec((None,) + b.shape[1:], lambda i: (i, 0, 0))],
        out_specs=pl.BlockSpec((None, m, n), lambda i: (i, 0, 0)),
        compiler_params=_params(("arbitrary",), 16 << 20),
        name="weight_product",
    )(a, b)


def _ab_in_kernel(x_ref, cos_ref, sin_ref, w_ref, wq_ref, qn_ref, kvn_ref,
                  c_ref, kr_ref, kcat_ref, q_ref, dq_ref, dk_ref, dv_ref, dkb_ref, dvb_ref,
                  *, ql, kl, dw, rope, n_mla, n_diff, dq_scale):
    xb = x_ref[...].astype(BF16)
    cos = cos_ref[...]
    sin = sin_ref[...]

    def mm(lo, width):
        return _nn(xb, w_ref[:, lo:lo + width])

    o_ckv, o_dq, o_dk, o_dv, o_kr = ql, ql + kl, ql + kl + dw, ql + kl + 2 * dw, ql + kl + 3 * dw
    c = _rms(mm(o_ckv, kl)) * kvn_ref[...]
    krr = mm(o_kr, 2 * LANES)
    kr = krr[:, :LANES] * cos + krr[:, LANES:] * sin
    c_ref[...] = c
    kr_ref[...] = kr[:, :rope]
    kcat_ref[:, :kl] = c.astype(BF16)
    kcat_ref[:, kl:] = kr.astype(BF16)
    dq = mm(o_dq, dw) * dq_scale
    dk = mm(o_dk, dw)
    dv = mm(o_dv, dw)
    dk_ref[...] = dk
    dv_ref[...] = dv
    for h in range(n_diff):
        sl = slice(h * LANES, (h + 1) * LANES)
        dq_ref[h] = dq[:, sl].astype(BF16)
        dkb_ref[h] = dk[:, sl].astype(BF16)
        dvb_ref[h] = dv[:, sl].astype(BF16)
    cq = (_rms(mm(0, ql)) * qn_ref[...]).astype(BF16)
    hw = kl + 2 * LANES
    for h in range(n_mla):
        qh = _nn(cq, wq_ref[:, h * hw:(h + 1) * hw])
        q_ref[h, :, :kl] = qh[:, :kl].astype(BF16)
        q_ref[h, :, kl:] = (qh[:, kl:kl + LANES] * cos + qh[:, kl + LANES:] * sin).astype(BF16)


def _ab_in(x, cos, sin, w_aug, wq_big, q_norm, kv_norm, *, tm, dims):
    bx, tx, d = x.shape
    ql, kl, dw, rope, n_mla, n_diff = dims["ql"], dims["kl"], dims["dw"], dims["rope"], dims["n_mla"], dims["n_diff"]
    row = lambda b, i: (b, i, 0)
    hrow = lambda b, i: (b, 0, i, 0)
    tab = lambda b, i: (i, 0)
    out_shape = (
        jax.ShapeDtypeStruct((bx, tx, kl), F32),
        jax.ShapeDtypeStruct((bx, tx, rope), F32),
        jax.ShapeDtypeStruct((bx, tx, kl + LANES), BF16),
        jax.ShapeDtypeStruct((bx, n_mla, tx, kl + LANES), BF16),
        jax.ShapeDtypeStruct((bx, n_diff, tx, LANES), BF16),
        jax.ShapeDtypeStruct((bx, tx, dw), F32),
        jax.ShapeDtypeStruct((bx, tx, dw), F32),
        jax.ShapeDtypeStruct((bx, n_diff, tx, LANES), BF16),
        jax.ShapeDtypeStruct((bx, n_diff, tx, LANES), BF16),
    )
    out_specs = (
        pl.BlockSpec((None, tm, kl), row),
        pl.BlockSpec((None, tm, rope), row),
        pl.BlockSpec((None, tm, kl + LANES), row),
        pl.BlockSpec((None, n_mla, tm, kl + LANES), hrow),
        pl.BlockSpec((None, n_diff, tm, LANES), hrow),
        pl.BlockSpec((None, tm, dw), row),
        pl.BlockSpec((None, tm, dw), row),
        pl.BlockSpec((None, n_diff, tm, LANES), hrow),
        pl.BlockSpec((None, n_diff, tm, LANES), hrow),
    )
    nbytes = (2 * tm * d * 4 + w_aug.size * 2 + wq_big.size * 2
              + 2 * tm * (kl * 4 + LANES * 4 + (kl + LANES) * 2 + n_mla * (kl + LANES) * 2
                          + 3 * n_diff * LANES * 2 + 2 * dw * 4)
              + 6 * tm * dw * 4 + (8 << 20))
    return pl.pallas_call(
        functools.partial(_ab_in_kernel, ql=ql, kl=kl, dw=dw, rope=rope, n_mla=n_mla, n_diff=n_diff,
                          dq_scale=dims["dq_scale"]),
        out_shape=out_shape,
        grid=(bx, tx // tm),
        in_specs=[pl.BlockSpec((None, tm, d), row),
                  pl.BlockSpec((tm, LANES), tab),
                  pl.BlockSpec((tm, LANES), tab),
                  _const_spec(w_aug), _const_spec(wq_big), _const_spec(q_norm), _const_spec(kv_norm)],
        out_specs=out_specs,
        compiler_params=_params(("parallel", "parallel"), nbytes),
        name="ab_in_proj",
    )(x, cos, sin, w_aug, wq_big, q_norm, kv_norm)


def _tri_schedule(nq):
    qi = [i for i in range(nq) for _ in range(i + 1)]
    kj = [j for i in range(nq) for j in range(i + 1)]
    return jnp.asarray(qi, jnp.int32), jnp.asarray(kj, jnp.int32)


def _mla_prompt_kernel(qi_ref, kj_ref, q_ref, k_ref, km_ref, o_ref, m_sc, l_sc, acc_sc,
                       *, n_heads, kl, n_meta, tq):
    step = pl.program_id(1)
    qi = qi_ref[step]
    kj = kj_ref[step]

    @pl.when(kj == 0)
    def _init():
        km = km_ref[...]
        vm = km_ref[:, :kl]
        valid = lax.broadcasted_iota(jnp.int32, (tq, MB), 1) < n_meta

        def body(i, carry):
            for u in range(HEAD_UNROLL):
                h = i * HEAD_UNROLL + u
                p = _softmax_first(jnp.where(valid, _nt(q_ref[h], km), NEG_INF), m_sc, l_sc, h)
                acc_sc[h] = _nn(p.astype(BF16), vm)
            return carry

        lax.fori_loop(0, n_heads // HEAD_UNROLL, body, 0)

    def block(visible):
        k = k_ref[...]
        v = k_ref[:, :kl]

        def body(i, carry):
            for u in range(HEAD_UNROLL):
                h = i * HEAD_UNROLL + u
                s = _nt(q_ref[h], k)
                if visible is not None:
                    s = jnp.where(visible, s, NEG_INF)
                alpha, p = _softmax_next(s, m_sc, l_sc, h)
                acc_sc[h] = _lane_tile(alpha, kl) * acc_sc[h] + _nn(p.astype(BF16), v)
            return carry

        lax.fori_loop(0, n_heads // HEAD_UNROLL, body, 0)

    @pl.when(kj < qi)
    def _below_diagonal():
        block(None)

    @pl.when(kj == qi)
    def _diagonal():
        row = lax.broadcasted_iota(jnp.int32, (tq, tq), 0)
        col = lax.broadcasted_iota(jnp.int32, (tq, tq), 1)
        block(col <= row)
        for h in range(n_heads):
            o_ref[:, h * kl:(h + 1) * kl] = (acc_sc[h] / _softmax_total(l_sc, h)).astype(o_ref.dtype)


def _mla_meta_kernel(q_ref, km_ref, prev_ref, o_ref, *, n_heads, kl, n_meta):
    del prev_ref
    km = km_ref[...]
    vm = km_ref[:, :kl]
    row = lax.broadcasted_iota(jnp.int32, (MB, MB), 0)
    col = lax.broadcasted_iota(jnp.int32, (MB, MB), 1)
    visible = (col <= row) & (col < n_meta)
    for h in range(n_heads):
        s = jnp.where(visible, _nt(q_ref[h], km), NEG_INF)
        p = jnp.exp2(s - jnp.max(s, axis=1, keepdims=True))
        o = _nn(p.astype(BF16), vm) / jnp.sum(p, axis=1, keepdims=True)
        o_ref[:, h * kl:(h + 1) * kl] = o.astype(o_ref.dtype)


def _mla_prompt(q, kcat, *, t_real, n_meta, tq):
    b, n_heads, tp, kw = q.shape
    kl = kw - LANES
    nq = t_real // tq
    qi, kj = _tri_schedule(nq)
    meta_blk = t_real // MB
    grid_spec = pltpu.PrefetchScalarGridSpec(
        num_scalar_prefetch=2,
        grid=(b, int(qi.shape[0])),
        in_specs=[pl.BlockSpec((None, n_heads, tq, kw), lambda bb, s, qi_r, kj_r: (bb, 0, qi_r[s], 0)),
                  pl.BlockSpec((None, tq, kw), lambda bb, s, qi_r, kj_r: (bb, kj_r[s], 0)),
                  pl.BlockSpec((None, MB, kw), lambda bb, s, qi_r, kj_r: (bb, meta_blk, 0))],
        out_specs=pl.BlockSpec((None, tq, n_heads * kl), lambda bb, s, qi_r, kj_r: (bb, qi_r[s], 0)),
        scratch_shapes=[pltpu.VMEM((n_heads, tq, LANES), F32), pltpu.VMEM((n_heads, tq, LANES), F32),
                        pltpu.VMEM((n_heads, tq, kl), F32)],
    )
    nbytes = (2 * n_heads * tq * kw * 2 + 2 * tq * kw * 2 + 2 * tq * n_heads * kl * 2
              + n_heads * tq * (kl + 2 * LANES) * 4 + 8 * tq * tq * 4 + (8 << 20))
    o_real = pl.pallas_call(
        functools.partial(_mla_prompt_kernel, n_heads=n_heads, kl=kl, n_meta=n_meta, tq=tq),
        out_shape=jax.ShapeDtypeStruct((b, tp, n_heads * kl), BF16),
        grid_spec=grid_spec,
        compiler_params=_params(("parallel", "arbitrary"), nbytes),
        name="mla_prompt_attention",
    )(qi, kj, q, kcat, kcat)
    return pl.pallas_call(
        functools.partial(_mla_meta_kernel, n_heads=n_heads, kl=kl, n_meta=n_meta),
        out_shape=jax.ShapeDtypeStruct((b, tp, n_heads * kl), BF16),
        grid=(b,),
        in_specs=[pl.BlockSpec((None, n_heads, MB, kw), lambda bb: (bb, 0, meta_blk, 0)),
                  pl.BlockSpec((None, MB, kw), lambda bb: (bb, meta_blk, 0)),
                  pl.BlockSpec(memory_space=pl.ANY)],
        out_specs=pl.BlockSpec((None, MB, n_heads * kl), lambda bb: (bb, meta_blk, 0)),
        input_output_aliases={2: 0},
        compiler_params=_params(("parallel",), 16 << 20),
        name="mla_meta_attention",
    )(q, kcat, o_real)


def _lambda_diff(lam_ref, lambda_init):
    lam = lam_ref[...]
    return (jnp.exp(jnp.sum(lam[0:1] * lam[1:2], axis=1, keepdims=True))
            - jnp.exp(jnp.sum(lam[2:3] * lam[3:4], axis=1, keepdims=True)) + lambda_init)


def _diff_split(q, qk):
    lane = lax.broadcasted_iota(jnp.int32, q.shape, 1)
    zero = jnp.zeros_like(q)
    return jnp.concatenate([jnp.where(lane < qk, q, zero), jnp.where(lane >= qk, q, zero)], axis=0)


def _diff_prompt_kernel(qi_ref, kj_ref, q_ref, k_ref, v_ref, km_ref, vm_ref, lam_ref, sub_ref,
                        o_ref, m_sc, l_sc, acc_sc, *, n_heads, qk, n_meta, tq, lambda_init):
    step = pl.program_id(1)
    qi = qi_ref[step]
    kj = kj_ref[step]

    @pl.when(kj == 0)
    def _init():
        valid = lax.broadcasted_iota(jnp.int32, (2 * tq, MB), 1) < n_meta

        def body(i, carry):
            for u in range(HEAD_UNROLL):
                h = i * HEAD_UNROLL + u
                s = jnp.where(valid, _nt(_diff_split(q_ref[h], qk), km_ref[h]), NEG_INF)
                p = _softmax_first(s, m_sc, l_sc, h)
                acc_sc[h] = _nn(p.astype(BF16), vm_ref[h])
            return carry

        lax.fori_loop(0, n_heads // HEAD_UNROLL, body, 0)

    def block(visible):
        def body(i, carry):
            for u in range(HEAD_UNROLL):
                h = i * HEAD_UNROLL + u
                s = _nt(_diff_split(q_ref[h], qk), k_ref[h])
                if visible is not None:
                    s = jnp.where(visible, s, NEG_INF)
                alpha, p = _softmax_next(s, m_sc, l_sc, h)
                acc_sc[h] = alpha * acc_sc[h] + _nn(p.astype(BF16), v_ref[h])
            return carry

        lax.fori_loop(0, n_heads // HEAD_UNROLL, body, 0)

    @pl.when(kj < qi)
    def _below_diagonal():
        block(None)

    @pl.when(kj == qi)
    def _diagonal():
        row = lax.broadcasted_iota(jnp.int32, (2 * tq, tq), 0)
        row = jnp.where(row >= tq, row - tq, row)
        col = lax.broadcasted_iota(jnp.int32, (2 * tq, tq), 1)
        block(col <= row)
        lam = _lambda_diff(lam_ref, lambda_init)
        for h in range(n_heads):
            o = acc_sc[h] / _softmax_total(l_sc, h)
            o = o[:tq] - lam * o[tq:]
            o = _rms(o) * sub_ref[...] * (1.0 - lambda_init)
            o_ref[:, h * LANES:(h + 1) * LANES] = o.astype(o_ref.dtype)


def _diff_meta_kernel(q_ref, km_ref, vm_ref, lam_ref, sub_ref, prev_ref, o_ref,
                      *, n_heads, qk, n_meta, lambda_init):
    del prev_ref
    row = lax.broadcasted_iota(jnp.int32, (2 * MB, MB), 0)
    row = jnp.where(row >= MB, row - MB, row)
    col = lax.broadcasted_iota(jnp.int32, (2 * MB, MB), 1)
    visible = (col <= row) & (col < n_meta)
    lam = _lambda_diff(lam_ref, lambda_init)
    for h in range(n_heads):
        s = jnp.where(visible, _nt(_diff_split(q_ref[h], qk), km_ref[h]), NEG_INF)
        p = jnp.exp2(s - jnp.max(s, axis=1, keepdims=True))
        o = _nn(p.astype(BF16), vm_ref[h]) / jnp.sum(p, axis=1, keepdims=True)
        o = o[:MB] - lam * o[MB:]
        o = _rms(o) * sub_ref[...] * (1.0 - lambda_init)
        o_ref[:, h * LANES:(h + 1) * LANES] = o.astype(o_ref.dtype)


def _diff_prompt(q, k, v, lam, subln, *, t_real, n_meta, tq, qk, lambda_init):
    b, n_heads, tp, _ = q.shape
    nq = t_real // tq
    qi, kj = _tri_schedule(nq)
    meta_blk = t_real // MB
    qmap = lambda bb, s, qi_r, kj_r: (bb, 0, qi_r[s], 0)
    kmap = lambda bb, s, qi_r, kj_r: (bb, 0, kj_r[s], 0)
    mmap = lambda bb, s, qi_r, kj_r: (bb, 0, meta_blk, 0)
    cmap = lambda bb, s, qi_r, kj_r: (0, 0)
    grid_spec = pltpu.PrefetchScalarGridSpec(
        num_scalar_prefetch=2,
        grid=(b, int(qi.shape[0])),
        in_specs=[pl.BlockSpec((None, n_heads, tq, LANES), qmap),
                  pl.BlockSpec((None, n_heads, tq, LANES), kmap),
                  pl.BlockSpec((None, n_heads, tq, LANES), kmap),
                  pl.BlockSpec((None, n_heads, MB, LANES), mmap),
                  pl.BlockSpec((None, n_heads, MB, LANES), mmap),
                  pl.BlockSpec(lam.shape, cmap),
                  pl.BlockSpec(subln.shape, cmap)],
        out_specs=pl.BlockSpec((None, tq, n_heads * LANES), lambda bb, s, qi_r, kj_r: (bb, qi_r[s], 0)),
        scratch_shapes=[pltpu.VMEM((n_heads, 2 * tq, LANES), F32), pltpu.VMEM((n_heads, 2 * tq, LANES), F32),
                        pltpu.VMEM((n_heads, 2 * tq, LANES), F32)],
    )
    nbytes = (6 * n_heads * tq * LANES * 2 + 2 * tq * n_heads * LANES * 2
              + n_heads * 2 * tq * 3 * LANES * 4 + 8 * 2 * tq * tq * 4 + (8 << 20))
    o_real = pl.pallas_call(
        functools.partial(_diff_prompt_kernel, n_heads=n_heads, qk=qk, n_meta=n_meta, tq=tq,
                          lambda_init=lambda_init),
        out_shape=jax.ShapeDtypeStruct((b, tp, n_heads * LANES), BF16),
        grid_spec=grid_spec,
        compiler_params=_params(("parallel", "arbitrary"), nbytes),
        name="diff_prompt_attention",
    )(qi, kj, q, k, v, k, v, lam, subln)
    mmap1 = lambda bb: (bb, 0, meta_blk, 0)
    return pl.pallas_call(
        functools.partial(_diff_meta_kernel, n_heads=n_heads, qk=qk, n_meta=n_meta, lambda_init=lambda_init),
        out_shape=jax.ShapeDtypeStruct((b, tp, n_heads * LANES), BF16),
        grid=(b,),
        in_specs=[pl.BlockSpec((None, n_heads, MB, LANES), mmap1),
                  pl.BlockSpec((None, n_heads, MB, LANES), mmap1),
                  pl.BlockSpec((None, n_heads, MB, LANES), mmap1),
                  pl.BlockSpec(lam.shape, lambda bb: (0, 0)),
                  pl.BlockSpec(subln.shape, lambda bb: (0, 0)),
                  pl.BlockSpec(memory_space=pl.ANY)],
        out_specs=pl.BlockSpec((None, MB, n_heads * LANES), lambda bb: (bb, meta_blk, 0)),
        input_output_aliases={5: 0},
        compiler_params=_params(("parallel",), 16 << 20),
        name="diff_meta_attention",
    )(q, k, v, lam, subln, o_real)


def _new_token_mask(rows, page, t_new):
    t = lax.broadcasted_iota(jnp.int32, (rows, page), 0) % t_new
    col = lax.broadcasted_iota(jnp.int32, (rows, page), 1)
    return col <= t


def _mla_pages_step(q_ref, cn_ref, krn_ref, lat_refs, rope_refs, o_ref, m_sc, l_sc, acc_sc, *, kl, rope, t_new):
    j = pl.program_id(1)
    q_lat = q_ref[:, :kl]
    q_pe = q_ref[:, kl:kl + rope]
    rows = q_ref.shape[0]
    page = cn_ref.shape[0]

    @pl.when(j == 0)
    def _init():
        cn = cn_ref[...].astype(BF16)
        s = _nt(q_lat, cn) + _nt(q_pe, krn_ref[...].astype(BF16))
        p = _softmax_first(jnp.where(_new_token_mask(rows, page, t_new), s, NEG_INF), m_sc, l_sc, 0)
        acc_sc[...] = _nn(p.astype(BF16), cn)

    lat = jnp.concatenate([r[...].astype(BF16) for r in lat_refs], axis=0)
    kr_t = jnp.concatenate([r[...].astype(BF16) for r in rope_refs], axis=1)
    alpha, p = _softmax_next(_nt(q_lat, lat) + _nn(q_pe, kr_t), m_sc, l_sc, 0)
    acc_sc[...] = _lane_tile(alpha, kl) * acc_sc[...] + _nn(p.astype(BF16), lat)

    @pl.when(j == pl.num_programs(1) - 1)
    def _finish():
        o_ref[...] = (acc_sc[...] / _softmax_total(l_sc, 0)).astype(o_ref.dtype)


def _page_specs(pool, li, n_pages, n_tab):
    blk = (None, None) + pool.shape[2:]
    specs = []
    for g in range(n_pages):
        specs.append(pl.BlockSpec(
            blk, lambda b, j, pt, g=g: (li, pt[b * n_tab + j * n_pages + g]) + (0,) * (pool.ndim - 2)))
    return specs


def _diff_pages_step(q_ref, kn_ref, vn_ref, lam_ref, sub_ref, k_refs, v_refs, o_ref, m_sc, l_sc, acc_sc,
                     *, n_heads, t_new, lambda_init):
    j = pl.program_id(1)
    q = q_ref[...]
    rows, width = q.shape
    page = kn_ref.shape[0]

    @pl.when(j == 0)
    def _init():
        s = _nt(q, kn_ref[...].astype(BF16))
        p = _softmax_first(jnp.where(_new_token_mask(rows, page, t_new), s, NEG_INF), m_sc, l_sc, 0)
        acc_sc[...] = _nn(p.astype(BF16), vn_ref[...].astype(BF16))

    k_t = jnp.concatenate([r[...].astype(BF16) for r in k_refs], axis=1)
    alpha, p = _softmax_next(_nn(q, k_t), m_sc, l_sc, 0)
    v = jnp.concatenate(
        [jnp.concatenate([r[pl.ds(h, page, stride=n_heads), :].astype(BF16) for h in range(n_heads)], axis=1)
         for r in v_refs], axis=0)
    acc_sc[...] = _lane_tile(alpha, width) * acc_sc[...] + _nn(p.astype(BF16), v)

    @pl.when(j == pl.num_programs(1) - 1)
    def _finish():
        lam = _lambda_diff(lam_ref, lambda_init)
        o = acc_sc[...] / _softmax_total(l_sc, 0)
        per_head = rows // n_heads
        for h in range(n_heads):
            blk = o[h * per_head:(h + 1) * per_head, h * LANES:(h + 1) * LANES]
            d = blk - lam * pltpu.roll(blk, per_head // 2, 0)
            d = _rms(d) * sub_ref[...] * (1.0 - lambda_init)
            o_ref[h * per_head:(h + 1) * per_head, :] = d.astype(o_ref.dtype)


def _paged_decode_kernel(pt_ref, qm_ref, cn_ref, krn_ref, qd_ref, kn_ref, vn_ref, lam_ref, sub_ref, *rest,
                         n_pages, kl, rope, n_heads, t_new, lambda_init):
    del pt_ref
    g = n_pages
    lat_refs, rope_refs, k_refs, v_refs = rest[:g], rest[g:2 * g], rest[2 * g:3 * g], rest[3 * g:4 * g]
    om_ref, od_ref, mm_sc, lm_sc, am_sc, md_sc, ld_sc, ad_sc = rest[4 * g:]
    _mla_pages_step(qm_ref, cn_ref, krn_ref, lat_refs, rope_refs, om_ref, mm_sc, lm_sc, am_sc,
                    kl=kl, rope=rope, t_new=t_new)
    _diff_pages_step(qd_ref, kn_ref, vn_ref, lam_ref, sub_ref, k_refs, v_refs, od_ref, md_sc, ld_sc, ad_sc,
                     n_heads=n_heads, t_new=t_new, lambda_init=lambda_init)


def _paged_decode(q_mla, c_new, kr_new, lat_pool, rope_pool_t, q_diff, k_new, v_new, lam, subln, k_pool_t, v_pool,
                  page_table, li, *, t_new, n_heads, pages_per_step, lambda_init):
    n_seq, rows, kw = q_mla.shape
    kl = kw - LANES
    rope, page = rope_pool_t.shape[2], rope_pool_t.shape[3]
    rows_d, width = q_diff.shape[1], q_diff.shape[2]
    n_tab = page_table.shape[1]
    g = pages_per_step
    seq3 = lambda b, j, pt: (b, 0, 0)
    cst = lambda b, j, pt: (0, 0)
    grid_spec = pltpu.PrefetchScalarGridSpec(
        num_scalar_prefetch=1,
        grid=(n_seq, n_tab // g),
        in_specs=[pl.BlockSpec((None, rows, kw), seq3),
                  pl.BlockSpec((None, page, kl), seq3),
                  pl.BlockSpec((None, page, rope), seq3),
                  pl.BlockSpec((None, rows_d, width), seq3),
                  pl.BlockSpec((None, page, width), seq3),
                  pl.BlockSpec((None, page, width), seq3),
                  pl.BlockSpec(lam.shape, cst),
                  pl.BlockSpec(subln.shape, cst)]
                 + _page_specs(lat_pool, li, g, n_tab) + _page_specs(rope_pool_t, li, g, n_tab)
                 + _page_specs(k_pool_t, li, g, n_tab) + _page_specs(v_pool, li, g, n_tab),
        out_specs=[pl.BlockSpec((None, rows, kl), seq3), pl.BlockSpec((None, rows_d, LANES), seq3)],
        scratch_shapes=[pltpu.VMEM((1, rows, LANES), F32), pltpu.VMEM((1, rows, LANES), F32),
                        pltpu.VMEM((rows, kl), F32),
                        pltpu.VMEM((1, rows_d, LANES), F32), pltpu.VMEM((1, rows_d, LANES), F32),
                        pltpu.VMEM((rows_d, width), F32)],
    )
    page_bytes = page * (kl + rope + 2 * width)
    nbytes = 2 * g * page_bytes * 4 + 2 * g * page_bytes * 2 + (8 << 20)
    return pl.pallas_call(
        functools.partial(_paged_decode_kernel, n_pages=g, kl=kl, rope=rope, n_heads=n_heads, t_new=t_new,
                          lambda_init=lambda_init),
        out_shape=[jax.ShapeDtypeStruct((n_seq, rows, kl), BF16), jax.ShapeDtypeStruct((n_seq, rows_d, LANES), F32)],
        grid_spec=grid_spec,
        compiler_params=_params(("parallel", "arbitrary"), nbytes),
        name="paged_attention",
    )(page_table.reshape(-1), q_mla, c_new, kr_new, q_diff, k_new, v_new, lam, subln,
      *([lat_pool] * g), *([rope_pool_t] * g), *([k_pool_t] * g), *([v_pool] * g))


def _post_kernel(*refs, n_act, alpha, tf):
    x_ref = refs[0]
    act_refs = refs[1:1 + n_act]
    w_refs = refs[1 + n_act:1 + 2 * n_act]
    g1_ref, b1_ref, g2_ref, b2_ref, w1_ref, w2_ref, o_ref = refs[1 + 2 * n_act:]
    mix = _nn(act_refs[0][...], w_refs[0][...])
    for a_ref, w_ref in zip(act_refs[1:], w_refs[1:]):
        mix = mix + _nn(a_ref[...], w_ref[...])
    x1 = _layer_norm(alpha * x_ref[...] + mix, g1_ref[...], b1_ref[...])
    x1b = x1.astype(BF16)
    d_ff = w1_ref.shape[1]
    acc = None
    for c in range(d_ff // tf):
        hid = jnp.maximum(_nn(x1b, w1_ref[:, c * tf:(c + 1) * tf]), 0.0)
        part = _nn((hid * hid).astype(BF16), w2_ref[c * tf:(c + 1) * tf, :])
        acc = part if acc is None else acc + part
    o_ref[...] = _layer_norm(alpha * x1 + acc, g2_ref[...], b2_ref[...])


def _post(x, acts, ws, g1, b1, g2, b2, w1, w2, *, tm, alpha, t_out=None):
    bx, tx, d = x.shape
    t_out = tx if t_out is None else t_out
    row = lambda b, i: (b, i, 0)
    tf = min(512, w1.shape[1])
    nbytes = (4 * tm * d * 4 + sum(2 * tm * a.shape[2] * 2 for a in acts) + sum(w.size * 2 for w in ws)
              + (w1.size + w2.size) * 2 + 6 * tm * d * 4 + 4 * tm * tf * 4 + (8 << 20))
    return pl.pallas_call(
        functools.partial(_post_kernel, n_act=len(acts), alpha=alpha, tf=tf),
        out_shape=jax.ShapeDtypeStruct((bx, t_out, d), F32),
        grid=(bx, t_out // tm),
        in_specs=[pl.BlockSpec((None, tm, d), row)]
                 + [pl.BlockSpec((None, tm, a.shape[2]), row) for a in acts]
                 + [_const_spec(w) for w in ws]
                 + [_const_spec(v) for v in (g1, b1, g2, b2, w1, w2)],
        out_specs=pl.BlockSpec((None, tm, d), row),
        compiler_params=_params(("parallel", "parallel"), nbytes),
        name="out_proj_mlp",
    )(x, *acts, *ws, g1, b1, g2, b2, w1, w2)


def _log_sigmoid(x):
    return -(jnp.maximum(-x, 0.0) + jnp.log1p(jnp.exp(-jnp.abs(x))))


def _cd_in_kernel(x_ref, cos_ref, sin_ref, w_ref, gb_ref, nrm_ref,
                  rq_ref, rk_ref, rv_ref, rg_ref, mq_ref, mk_ref, mv_ref, mg_ref, gt_ref,
                  *, qw, vw, n_ml, rk_scale, mk_scale):
    xb = x_ref[...].astype(BF16)
    cos = cos_ref[...]
    sin = sin_ref[...]

    def mm(lo, width):
        return _nn(xb, w_ref[:, lo:lo + width])

    rq_ref[...] = mm(0, qw) * cos + mm(qw, qw) * sin
    rk_ref[...] = (mm(2 * qw, qw) * cos + mm(3 * qw, qw) * sin) * rk_scale
    o = 4 * qw
    rv_ref[...] = mm(o, vw).astype(BF16)
    rg = mm(o + vw, vw)
    rg_ref[...] = rg * jax.nn.sigmoid(rg)
    o = o + 2 * vw
    mq_ref[...] = mm(o, qw)
    mk_ref[...] = mm(o + qw, qw) * mk_scale
    o = o + 2 * qw
    mv_ref[...] = mm(o, vw).astype(BF16)
    mg_ref[...] = jax.nn.sigmoid(mm(o + vw, vw)) * nrm_ref[...]
    pre = mm(o + 2 * vw, LANES) + gb_ref[...]
    lane = lax.broadcasted_iota(jnp.int32, pre.shape, 1)
    gt_ref[...] = jnp.where(lane < n_ml, pre, _log_sigmoid(pre))


def _cd_in(x, cos, sin, w_aug, gate_bias, ml_norm, *, tm, dims):
    bx, tx, d = x.shape
    qw, vw = dims["cd_qw"], dims["cd_vw"]
    row = lambda b, i: (b, i, 0)
    tab = lambda b, i: (i, 0)
    shapes = [(qw, F32), (qw, F32), (vw, BF16), (vw, F32), (qw, F32), (qw, F32), (vw, BF16), (vw, F32), (LANES, F32)]
    nbytes = (2 * tm * d * 4 + w_aug.size * 2 + 2 * tm * sum(w * (4 if t == F32 else 2) for w, t in shapes)
              + 8 * tm * vw * 4 + (8 << 20))
    return pl.pallas_call(
        functools.partial(_cd_in_kernel, qw=qw, vw=vw, n_ml=dims["n_ml"],
                          rk_scale=dims["rk_scale"], mk_scale=dims["mk_scale"]),
        out_shape=tuple(jax.ShapeDtypeStruct((bx, tx, w), t) for w, t in shapes),
        grid=(bx, tx // tm),
        in_specs=[pl.BlockSpec((None, tm, d), row),
                  pl.BlockSpec((tm, qw), tab), pl.BlockSpec((tm, qw), tab),
                  _const_spec(w_aug), _const_spec(gate_bias), _const_spec(ml_norm)],
        out_specs=tuple(pl.BlockSpec((None, tm, w), row) for w, _ in shapes),
        compiler_params=_params(("parallel", "parallel"), nbytes),
        name="cd_in_proj",
    )(x, cos, sin, w_aug, gate_bias, ml_norm)


def _split_bf16(x):
    hi = x.astype(BF16)
    lo = (x - hi.astype(F32)).astype(BF16)
    return hi, lo


def _chunk_kernel(ids_ref, rq_ref, rk_ref, rv_ref, rg_ref, mq_ref, mk_ref, mv_ref, mg_ref, gt_ref,
                  dec_ref, lgr_ref, lgc_ref, s0_ref, c0_ref, n0_ref, m0_ref,
                  o_ref, s_out, c_out, n_out, m_out,
                  s_sc, c_sc, n_sc, m_sc,
                  *, n_heads, dk, dv, first_valid, n_seq_blk):
    del ids_ref
    ci = pl.program_id(1)

    @pl.when(ci == 0)
    def _load_state():
        s_sc[...] = s0_ref[...]
        c_sc[...] = c0_ref[...]
        n_sc[...] = n0_ref[...]
        m_sc[...] = m0_ref[...]

    n_valid = jnp.where(ci == 0, first_valid, MB)
    for bi in range(n_seq_blk):
        _chunk_step(bi, n_valid, rq_ref, rk_ref, rv_ref, rg_ref, mq_ref, mk_ref, mv_ref, mg_ref, gt_ref,
                    dec_ref, lgr_ref, lgc_ref, o_ref, s_sc, c_sc, n_sc, m_sc, n_heads=n_heads, dk=dk, dv=dv)

    @pl.when(ci == pl.num_programs(1) - 1)
    def _store_state():
        s_out[...] = s_sc[...]
        c_out[...] = c_sc[...]
        n_out[...] = n_sc[...]
        m_out[...] = m_sc[...]


def _chunk_step(bi, n_valid, rq_ref, rk_ref, rv_ref, rg_ref, mq_ref, mk_ref, mv_ref, mg_ref, gt_ref,
                dec_ref, lgr_ref, lgc_ref, o_ref, s_sc, c_sc, n_sc, m_sc, *, n_heads, dk, dv):
    qw = n_heads * dk
    idx_c = lax.broadcasted_iota(jnp.int32, (MB, 1), 0)
    live_c = idx_c < n_valid
    lane_q = lax.broadcasted_iota(jnp.int32, (MB, qw), 1)
    t_f = idx_c.astype(F32)
    nv_f = n_valid.astype(F32)

    lgr = lgr_ref[...]
    rq = rq_ref[bi]
    rk = jnp.where(live_c, rk_ref[bi], 0.0)
    rv = rv_ref[bi]
    q_dec = rq * jnp.exp(lgr * (t_f + 1.0))
    k_dec = (rk * jnp.exp(lgr * jnp.where(live_c, nv_f - 1.0 - t_f, 0.0))).astype(BF16)
    rk_b = rk.astype(BF16)
    s_all = s_sc[bi]
    s_b = s_all.astype(BF16)
    upd = _tn(k_dec, rv)
    s_dec = jnp.exp(lgc_ref[...] * nv_f)
    for h in range(n_heads):
        head = (lane_q >= h * dk) & (lane_q < (h + 1) * dk)
        inner = _nt(jnp.where(head, rq, 0.0).astype(BF16), rk_b) * dec_ref[h]
        o = _nn(inner.astype(BF16), rv[:, h * dv:(h + 1) * dv]) + _nn(jnp.where(head, q_dec, 0.0).astype(BF16), s_b)
        o = _head_norm(o) * rg_ref[bi, :, h * dv:(h + 1) * dv]
        o_ref[bi, :, h * dv:(h + 1) * dv] = o.astype(o_ref.dtype)
        s_sc[bi, h * dk:(h + 1) * dk, :] = (s_dec[h * dk:(h + 1) * dk] * s_all[h * dk:(h + 1) * dk]
                                            + upd[h * dk:(h + 1) * dk, h * dv:(h + 1) * dv])

    gt = gt_ref[bi]
    lane_g = lax.broadcasted_iota(jnp.int32, gt.shape, 1)
    gt = jnp.where(live_c, gt, jnp.where(lane_g < n_heads, NEG_INF, 0.0))
    row_i = lax.broadcasted_iota(jnp.int32, (MB, MB), 0)
    col_i = lax.broadcasted_iota(jnp.int32, (MB, MB), 1)
    causal = col_i <= row_i
    tri = causal.astype(BF16)
    g_hi, g_lo = _split_bf16(gt)
    cum_c = _nn(tri, g_hi) + _nn(tri, g_lo)
    gt_r = gt.T
    cum_r = cum_c.T
    mq = mq_ref[bi]
    mk = jnp.where(live_c, mk_ref[bi], 0.0)
    mk_b = mk.astype(BF16)
    mv = mv_ref[bi]
    c_cat = c_sc[bi]
    c_b = c_cat.astype(BF16)
    n_cat = n_sc[bi]
    m_all = m_sc[bi]
    c_new = jnp.zeros_like(c_cat)
    n_new = jnp.zeros_like(n_cat)
    g_keep = jnp.zeros_like(n_cat)
    m_new = jnp.zeros_like(m_all)
    lane_m = lax.broadcasted_iota(jnp.int32, m_all.shape, 1)
    lane_n = lax.broadcasted_iota(jnp.int32, n_cat.shape, 1)
    for h in range(n_heads):
        head = (lane_q >= h * dk) & (lane_q < (h + 1) * dk)
        head_n = (lane_n >= h * dk) & (lane_n < (h + 1) * dk)
        b_c = cum_c[:, n_heads + h:n_heads + h + 1]
        i_c = gt[:, h:h + 1]
        b_r = cum_r[n_heads + h:n_heads + h + 1, :]
        i_r = gt_r[h:h + 1, :]
        m_prev = m_all[:, h:h + 1]
        d = jnp.where(causal, b_c - b_r + i_r, NEG_INF)
        g = b_c + m_prev
        m_t = jnp.maximum(g, jnp.max(d, axis=1, keepdims=True))
        w = jnp.exp(d - m_t)
        gs = jnp.exp(g - m_t)
        qh = jnp.where(head, mq, 0.0)
        qh_b = qh.astype(BF16)
        a = _nt(qh_b, mk_b) * w
        num = _nn(a.astype(BF16), mv[:, h * dv:(h + 1) * dv]) + gs * _nt(qh_b, c_b)
        den = jnp.sum(a, axis=1, keepdims=True) + gs * jnp.sum(qh * n_cat, axis=1, keepdims=True)
        hid = num / jnp.maximum(jnp.abs(den), jnp.exp(-m_t))
        hid = _head_norm(hid) * mg_ref[bi, :, h * dv:(h + 1) * dv]
        o_ref[bi, :, n_heads * dv + h * dv:n_heads * dv + (h + 1) * dv] = hid.astype(o_ref.dtype)
        m_last = m_t[MB - 1:MB]
        w_last = jnp.exp(b_c[MB - 1:MB] - b_c + i_c - m_last)
        g_last = gs[MB - 1:MB]
        kh = jnp.where(head, mk, 0.0)
        c_new = c_new + _tn((mv[:, h * dv:(h + 1) * dv].astype(F32) * w_last).astype(BF16), kh.astype(BF16))
        n_new = n_new + jnp.sum(kh * w_last, axis=0, keepdims=True)
        g_keep = jnp.where(head_n, g_last, g_keep)
        m_new = jnp.where(lane_m == h, m_last, m_new)
    c_sc[bi] = g_keep * c_cat + c_new
    n_sc[bi] = g_keep * n_cat + n_new
    m_sc[bi] = m_new


def _chunk_scan(rq, rk, rv, rg, mq, mk, mv, mg, gt, dec, lgr, lgc, s0, c0, n0, m0,
                *, n_heads, dk, dv, blocks, first_valid, n_seq_blk):
    bx, tx, qw = rq.shape
    vw = rv.shape[2]
    nb = n_seq_blk
    assert bx % nb == 0
    blk_ids = jnp.asarray(blocks, jnp.int32)
    row = lambda b, c, ids: (b, ids[c], 0)
    st = lambda b, c, ids: (b, 0, 0)
    cst3 = lambda b, c, ids: (0, 0, 0)
    cst2 = lambda b, c, ids: (0, 0)
    state_shapes = [(qw, dv), (dv, qw), (1, qw), (1, LANES)]
    grid_spec = pltpu.PrefetchScalarGridSpec(
        num_scalar_prefetch=1,
        grid=(bx // nb, len(blocks)),
        in_specs=[pl.BlockSpec((nb, MB, qw), row), pl.BlockSpec((nb, MB, qw), row),
                  pl.BlockSpec((nb, MB, vw), row), pl.BlockSpec((nb, MB, vw), row),
                  pl.BlockSpec((nb, MB, qw), row), pl.BlockSpec((nb, MB, qw), row),
                  pl.BlockSpec((nb, MB, vw), row), pl.BlockSpec((nb, MB, vw), row),
                  pl.BlockSpec((nb, MB, LANES), row),
                  pl.BlockSpec(dec.shape, cst3), pl.BlockSpec(lgr.shape, cst2), pl.BlockSpec(lgc.shape, cst2)]
                 + [pl.BlockSpec((nb,) + s, st) for s in state_shapes],
        out_specs=[pl.BlockSpec((nb, MB, 2 * vw), row)] + [pl.BlockSpec((nb,) + s, st) for s in state_shapes],
        scratch_shapes=[pltpu.VMEM((nb,) + s, F32) for s in state_shapes],
    )
    out_shape = [jax.ShapeDtypeStruct((bx, tx, 2 * vw), BF16)] + [jax.ShapeDtypeStruct((bx,) + s, F32) for s in state_shapes]
    return pl.pallas_call(
        functools.partial(_chunk_kernel, n_heads=n_heads, dk=dk, dv=dv, first_valid=first_valid, n_seq_blk=nb),
        out_shape=out_shape,
        grid_spec=grid_spec,
        compiler_params=_params(("parallel", "arbitrary"), 32 << 20),
        name="retention_mlstm_chunks",
    )(blk_ids, rq, rk, rv, rg, mq, mk, mv, mg, gt, dec, lgr, lgc, s0, c0, n0, m0)


def _rope_tables(pos, half, reps, width):
    freqs = ROPE_BASE ** (-jnp.arange(half, dtype=F32) / half)
    ang = pos.astype(F32)[:, None] * freqs[None, :]
    cos, sin = jnp.cos(ang), jnp.sin(ang)
    cos = jnp.tile(jnp.concatenate([cos, cos], axis=1), (1, reps))
    sin = jnp.tile(jnp.concatenate([-sin, sin], axis=1), (1, reps))
    pad = width - cos.shape[1]
    return jnp.pad(cos, ((0, 0), (0, pad))), jnp.pad(sin, ((0, 0), (0, pad)))


def _swap_halves(w, group):
    shp = w.shape
    w = w.reshape(shp[:-1] + (shp[-1] // group, 2, group // 2))
    return w[..., ::-1, :].reshape(shp)


def _pad_cols(w, width):
    return jnp.pad(w, ((0, 0), (0, width - w.shape[1])))


def _pad_page(rows, n_seq, t_new, page):
    w = rows.shape[-1]
    return jnp.pad(rows.reshape(n_seq, t_new, w), ((0, 0), (0, page - t_new), (0, 0)))


def _meta_first(a, t_real, n_meta):
    return jnp.concatenate([a[:, t_real:t_real + n_meta], a[:, :t_real]], axis=1)


def kernel(x_prompt, x_sample, cache_mla_latent, cache_mla_rope, cache_diff_k, cache_diff_v, state_ret, state_mlstm_C, state_mlstm_n, state_mlstm_m, page_table, meta_tokens, ab_w_in, mla_q_norm, mla_w_uq, mla_kv_norm, mla_w_uk, mla_w_uv, diff_lambda_q1, diff_lambda_k1, diff_lambda_q2, diff_lambda_k2, diff_subln, ab_w_out, cd_w_in, ml_b_i, ml_b_f, ml_norm, cd_w_out, ln1_g, ln1_b, ln2_g, ln2_b, mlp_w1, mlp_w2):
    b, t_real, d = x_prompt.shape
    n_seq, t_new, _ = x_sample.shape
    n_meta = meta_tokens.shape[0]
    depth = ln1_g.shape[0]
    alpha = (2 * depth) ** 0.25
    page = cache_mla_latent.shape[2]
    past_len = page_table.shape[1] * page
    ql, kl, rope = mla_q_norm.shape[1], mla_kv_norm.shape[1], cache_mla_rope.shape[3]
    n_mla, nope, mla_v = mla_w_uk.shape[2], mla_w_uk.shape[3], mla_w_uv.shape[3]
    n_diff, diff_qk, diff_v = cache_diff_k.shape[3], cache_diff_k.shape[5], cache_diff_v.shape[4]
    n_ret, ret_dk, ret_dv = state_ret.shape[2], state_ret.shape[3], state_ret.shape[4]
    n_ml, ml_dv, ml_dk = state_mlstm_C.shape[2], state_mlstm_C.shape[3], state_mlstm_C.shape[4]
    assert 2 * diff_qk == LANES and diff_v == LANES and n_meta <= MB and t_real % MB == 0 and page == MB
    assert (n_ret, ret_dk, ret_dv) == (n_ml, ml_dk, ml_dv) and ret_dv == LANES and t_new <= MB
    dw = n_diff * LANES
    dims = dict(ql=ql, kl=kl, dw=dw, rope=rope, n_mla=n_mla, n_diff=n_diff, dq_scale=diff_qk ** -0.5 * LOG2E,
                cd_qw=n_ret * ret_dk, cd_vw=n_ret * ret_dv, n_ml=n_ml, rk_scale=ret_dk ** -0.5, mk_scale=ml_dk ** -0.5)
    tp = t_real + MB
    n_rows_s = n_seq * t_new
    tm_p = _tile(tp, 640)
    tm_s = _tile(n_rows_s, 512)
    tq = _tile(t_real, 512)

    xp = jnp.concatenate([x_prompt, jnp.broadcast_to(meta_tokens.astype(x_prompt.dtype)[None], (b, n_meta, d)),
                          jnp.zeros((b, MB - n_meta, d), x_prompt.dtype)], axis=1)
    xs = x_sample.reshape(1, n_rows_s, d)
    pos_p = jnp.concatenate([n_meta + jnp.arange(t_real), jnp.arange(MB)])
    pos_s = past_len + (jnp.arange(n_rows_s) % t_new)

    prompt_out, sample_out = {}, {}
    for layer in range(depth):
        g1, b1, g2, b2 = (v[layer][None] for v in (ln1_g, ln1_b, ln2_g, ln2_b))
        w1, w2 = mlp_w1[layer].astype(BF16), mlp_w2[layer].astype(BF16)
        last = layer == depth - 1
        if layer % 2 == 0:
            li = layer // 2
            lambda_init = 0.8 - 0.6 * math.exp(-0.3 * layer)
            scale = (nope + rope) ** -0.5 * LOG2E
            w_in = ab_w_in[li]
            o = [0, ql, ql + kl, ql + kl + rope, ql + kl + rope + dw, ql + kl + rope + 2 * dw, ql + kl + rope + 3 * dw]
            w_cq, w_ckv, w_kr, w_dq, w_dk, w_dv = (w_in[:, o[i]:o[i + 1]] for i in range(6))
            w_aug = jnp.concatenate([w_cq, w_ckv, w_dq, w_dk, w_dv, _pad_cols(w_kr, LANES),
                                     _pad_cols(_swap_halves(w_kr, rope), LANES)], axis=1).astype(BF16)
            w_uq = mla_w_uq[li].reshape(ql, n_mla, nope + rope)
            w_abs = _wprod(w_uq[:, :, :nope].transpose(1, 0, 2), mla_w_uk[li].transpose(1, 0, 2), True)
            w_pe = w_uq[:, :, nope:].transpose(1, 0, 2)
            pe_pad = ((0, 0), (0, 0), (0, LANES - rope))
            wq_big = (jnp.concatenate([w_abs, jnp.pad(w_pe, pe_pad), jnp.pad(_swap_halves(w_pe, rope), pe_pad)], axis=2)
                      * scale).transpose(1, 0, 2).reshape(ql, n_mla * (kl + 2 * LANES)).astype(BF16)
            w_out = ab_w_out[li]
            w_comb = _wprod(mla_w_uv[li].transpose(1, 0, 2), w_out[:n_mla * mla_v].reshape(n_mla, mla_v, d), False)
            w_comb = w_comb.reshape(n_mla * kl, d).astype(BF16)
            w_out_diff = w_out[n_mla * mla_v:].astype(BF16)
            qn, kvn = mla_q_norm[li][None], mla_kv_norm[li][None]
            lam = jnp.stack([diff_lambda_q1[li], diff_lambda_k1[li], diff_lambda_q2[li], diff_lambda_k2[li]])
            subln = diff_subln[li][None]

            cos_p, sin_p = _rope_tables(pos_p, rope // 2, 1, LANES)
            cos_s, sin_s = _rope_tables(pos_s, rope // 2, 1, LANES)
            c_p, kr_p, kcat_p, q_p, dq_p, dk_p, dv_p, dkb_p, dvb_p = _ab_in(
                xp, cos_p, sin_p, w_aug, wq_big, qn, kvn, tm=tm_p, dims=dims)
            c_s, kr_s, _, q_s, dq_s, dk_s, dv_s, _, _ = _ab_in(
                xs, cos_s, sin_s, w_aug, wq_big, qn, kvn, tm=tm_s, dims=dims)

            o_lat_p = _mla_prompt(q_p, kcat_p, t_real=t_real, n_meta=n_meta, tq=tq)
            o_diff_p = _diff_prompt(dq_p, dkb_p, dvb_p, lam, subln, t_real=t_real, n_meta=n_meta, tq=tq,
                                    qk=diff_qk, lambda_init=lambda_init)

            kw = kl + LANES
            q_dec = q_s.reshape(n_mla, n_seq, t_new, kw).transpose(1, 0, 2, 3).reshape(n_seq, n_mla * t_new, kw)
            dq5 = dq_s.reshape(n_diff, n_seq, t_new, 2, diff_qk).transpose(1, 0, 3, 2, 4)
            eye_h = jnp.eye(n_diff, dtype=BF16)
            eye_m = jnp.eye(2, dtype=BF16)
            q_bd = (dq5[:, :, :, :, None, None, :] * eye_h[None, :, None, None, :, None, None]
                    * eye_m[None, None, :, None, None, :, None]).reshape(n_seq, n_diff * 2 * t_new, dw)
            k_pool_t = cache_diff_k.transpose(0, 1, 3, 4, 5, 2).reshape(cache_diff_k.shape[:2] + (dw, page))
            v_pool = cache_diff_v.reshape(cache_diff_v.shape[:2] + (page * n_diff, diff_v))
            o_lat_s, o_diff_s = _paged_decode(
                q_dec, _pad_page(c_s[0], n_seq, t_new, page), _pad_page(kr_s[0], n_seq, t_new, page),
                cache_mla_latent, cache_mla_rope.transpose(0, 1, 3, 2),
                q_bd, _pad_page(dk_s[0], n_seq, t_new, page), _pad_page(dv_s[0], n_seq, t_new, page),
                lam, subln, k_pool_t, v_pool, page_table, li, t_new=t_new, n_heads=n_diff,
                pages_per_step=_tile_pages(page_table.shape[1], 16), lambda_init=lambda_init)
            o_lat_s = (o_lat_s.reshape(n_seq, n_mla, t_new, kl).transpose(0, 2, 1, 3)
                       .reshape(1, n_rows_s, n_mla * kl))
            o_diff_s = (o_diff_s.reshape(n_seq, n_diff, 2, t_new, LANES)[:, :, 0].transpose(0, 2, 1, 3)
                        .reshape(1, n_rows_s, dw).astype(BF16))

            acts_p, acts_s, ws = [o_lat_p, o_diff_p], [o_lat_s, o_diff_s], [w_comb, w_out_diff]
            for name, rows_p, rows_s in (("c", c_p, c_s), ("kr", kr_p, kr_s), ("dk", dk_p, dk_s), ("dv", dv_p, dv_s)):
                prompt_out.setdefault(name, []).append(_meta_first(rows_p, t_real, n_meta))
                sample_out.setdefault(name, []).append(rows_s.reshape(n_seq, t_new, -1))
        else:
            lj = layer // 2
            qw, vw = dims["cd_qw"], dims["cd_vw"]
            w_in = cd_w_in[lj]
            widths = [qw, qw, vw, vw, qw, qw, vw, vw, n_ml, n_ml]
            offs = [0]
            for wdt in widths:
                offs.append(offs[-1] + wdt)
            w_rq, w_rk, w_rv, w_rg, w_mq, w_mk, w_mv, w_mo, w_mi, w_mf = (w_in[:, offs[i]:offs[i + 1]] for i in range(10))
            w_aug = jnp.concatenate([w_rq, _swap_halves(w_rq, ret_dk), w_rk, _swap_halves(w_rk, ret_dk), w_rv, w_rg,
                                     w_mq, w_mk, w_mv, w_mo,
                                     _pad_cols(jnp.concatenate([w_mi, w_mf], axis=1), LANES)], axis=1).astype(BF16)
            gate_bias = _pad_cols(jnp.concatenate([ml_b_i[lj], ml_b_f[lj]])[None], LANES)
            nrm = ml_norm[lj][None]
            w_out = cd_w_out[lj].astype(BF16)
            log_gamma = jnp.log1p(-jnp.power(2.0, -5.0 - jnp.arange(n_ret, dtype=F32)))
            idx = jnp.arange(MB, dtype=F32)
            diff_ts = idx[:, None] - idx[None, :]
            dec = jnp.where(diff_ts >= 0, jnp.exp(log_gamma[:, None, None] * jnp.maximum(diff_ts, 0.0)), 0.0)
            lgr = jnp.repeat(log_gamma, ret_dk)[None]
            lgc = jnp.repeat(log_gamma, ret_dk)[:, None]

            cos_p, sin_p = _rope_tables(pos_p, ret_dk // 2, n_ret, qw)
            cos_s, sin_s = _rope_tables(pos_s, ret_dk // 2, n_ret, qw)
            rows_p = _cd_in(xp, cos_p, sin_p, w_aug, gate_bias, nrm, tm=tm_p, dims=dims)
            rows_s = _cd_in(xs, cos_s, sin_s, w_aug, gate_bias, nrm, tm=tm_s, dims=dims)

            zeros = lambda *s: jnp.zeros(s, F32)
            chunk = functools.partial(_chunk_scan, n_heads=n_ret, dk=ret_dk, dv=ret_dv)
            o_p, s_p, c_p2, n_p, m_p = chunk(
                *rows_p, dec, lgr, lgc, zeros(b, qw, ret_dv), zeros(b, ret_dv, qw), zeros(b, 1, qw), zeros(b, 1, LANES),
                blocks=[t_real // MB] + list(range(t_real // MB)), first_valid=n_meta,
                n_seq_blk=1)
            rows_s_pad = [_pad_page(r[0], n_seq, t_new, MB) for r in rows_s]
            c0 = state_mlstm_C[lj].transpose(0, 2, 1, 3).reshape(n_seq, ml_dv, qw)
            o_s, s_s, c_s2, n_s, m_s = chunk(
                *rows_s_pad, dec, lgr, lgc, state_ret[lj].reshape(n_seq, qw, ret_dv), c0,
                state_mlstm_n[lj].reshape(n_seq, 1, qw), _pad_cols(state_mlstm_m[lj], LANES)[:, None],
                blocks=[0], first_valid=t_new, n_seq_blk=1)
            o_s = o_s[:, :t_new].reshape(1, n_rows_s, 2 * vw)

            acts_p, acts_s, ws = [o_p], [o_s], [w_out]
            for dst, s_, c_, n_, m_, nb in ((prompt_out, s_p, c_p2, n_p, m_p, b), (sample_out, s_s, c_s2, n_s, m_s, n_seq)):
                dst.setdefault("s", []).append(s_.reshape(nb, n_ret, ret_dk, ret_dv))
                dst.setdefault("C", []).append(c_.reshape(nb, ml_dv, n_ml, ml_dk).transpose(0, 2, 1, 3))
                dst.setdefault("n", []).append(n_.reshape(nb, n_ml, ml_dk))
                dst.setdefault("m", []).append(m_[:, 0, :n_ml])

        if last:
            xp = _post(xp, acts_p, ws, g1, b1, g2, b2, w1, w2, tm=_tile(t_real, 512), alpha=alpha, t_out=t_real)
        else:
            xp = _post(xp, acts_p, ws, g1, b1, g2, b2, w1, w2, tm=tm_p, alpha=alpha)
        xs = _post(xs, acts_s, ws, g1, b1, g2, b2, w1, w2, tm=tm_s, alpha=alpha)

    y_prompt = xp
    y_sample = xs.reshape(n_seq, t_new, d)
    k_shape_p = (b, n_meta + t_real, n_diff, 2, diff_qk)
    v_shape_p = (b, n_meta + t_real, n_diff, diff_v)
    k_shape_s = (n_seq, t_new, n_diff, 2, diff_qk)
    v_shape_s = (n_seq, t_new, n_diff, diff_v)
    return (y_prompt, y_sample,
            jnp.stack(prompt_out["c"]), jnp.stack(prompt_out["kr"]),
            jnp.stack([a.reshape(k_shape_p) for a in prompt_out["dk"]]),
            jnp.stack([a.reshape(v_shape_p) for a in prompt_out["dv"]]),
            jnp.stack(prompt_out["s"]), jnp.stack(prompt_out["C"]), jnp.stack(prompt_out["n"]), jnp.stack(prompt_out["m"]),
            jnp.stack(sample_out["c"]), jnp.stack(sample_out["kr"]),
            jnp.stack([a.reshape(k_shape_s) for a in sample_out["dk"]]),
            jnp.stack([a.reshape(v_shape_s) for a in sample_out["dv"]]),
            jnp.stack(sample_out["s"]), jnp.stack(sample_out["C"]), jnp.stack(sample_out["n"]), jnp.stack(sample_out["m"]))


def _tile_pages(n_pages, pref):
    g = min(pref, n_pages)
    while n_pages % g:
        g -= 1
    return g
```

```python
import functools
import math

import jax
import jax.numpy as jnp
from jax import lax
from jax.experimental import pallas as pl
from jax.experimental.pallas import tpu as pltpu

F32 = jnp.float32
BF16 = jnp.bfloat16

ROPE_BASE = 10000.0
NEG_INF = -1e30
EPS = 1e-5

LANES = 128
MB = 128
HEAD_UNROLL = 4
VMEM_CAP_BYTES = 56 << 20


def _vmem(nbytes):
    return int(min(VMEM_CAP_BYTES, max(16 << 20, nbytes)))


def _params(sem, nbytes):
    return pltpu.CompilerParams(dimension_semantics=sem, vmem_limit_bytes=_vmem(nbytes))


def _const_spec(arr):
    nd = arr.ndim
    return pl.BlockSpec(arr.shape, lambda *_: (0,) * nd, pipeline_mode=pl.Buffered(1))


def _tile(n, pref):
    if n <= pref:
        return n
    best = MB
    for t in range(MB, pref + 1, MB):
        if n % t == 0:
            best = t
    assert n % best == 0, (n, pref)
    return best


def _nt(a, b):
    return lax.dot_general(a, b, (((1,), (1,)), ((), ())), preferred_element_type=F32)


def _nn(a, b):
    return jnp.dot(a, b, preferred_element_type=F32)


def _tn(a, b):
    return _nn(a.T, b)


def _rms(x):
    return x * lax.rsqrt(jnp.mean(x * x, axis=-1, keepdims=True) + EPS)


def _layer_norm(x, g, b):
    mu = jnp.mean(x, axis=-1, keepdims=True)
    xc = x - mu
    var = jnp.mean(xc * xc, axis=-1, keepdims=True)
    return xc * lax.rsqrt(var + EPS) * g + b


def _head_norm(x):
    mu = jnp.mean(x, axis=-1, keepdims=True)
    xc = x - mu
    var = jnp.mean(xc * xc, axis=-1, keepdims=True)
    return xc * lax.rsqrt(var + EPS)


LOG2E = math.log2(math.e)
EXP2_HEADROOM = 64.0
NORM_SLACK = 1.0 + 2.0 ** -6


def _lane_tile(x, width):
    reps = width // LANES
    return x if reps == 1 else jnp.concatenate([x] * reps, axis=1)


def _lane_fold(p):
    part = p[:, :LANES]
    for i in range(1, p.shape[1] // LANES):
        part = part + p[:, i * LANES:(i + 1) * LANES]
    return part


def _softmax_first(s, m_ref, l_ref, idx):
    m = jnp.broadcast_to(jnp.max(s, axis=1, keepdims=True), (s.shape[0], LANES))
    p = jnp.exp2(s - _lane_tile(m, s.shape[1]))
    m_ref[idx] = m
    l_ref[idx] = _lane_fold(p)
    return p


def _softmax_next(s, m_ref, l_ref, idx):
    m_prev = m_ref[idx]
    m_new = jnp.maximum(m_prev, jnp.max(s, axis=1, keepdims=True))
    alpha = jnp.exp2(m_prev - m_new)
    p = jnp.exp2(s - _lane_tile(m_new, s.shape[1]))
    l_ref[idx] = alpha * l_ref[idx] + _lane_fold(p)
    m_ref[idx] = m_new
    return alpha, p


def _softmax_total(l_ref, idx):
    return jnp.sum(l_ref[idx], axis=1, keepdims=True)


def _wprod_kernel(a_ref, b_ref, o_ref, *, trans_b):
    a = a_ref[...].astype(BF16)
    b = b_ref[...].astype(BF16)
    o_ref[...] = _nt(a, b) if trans_b else _nn(a, b)


def _wprod(a, b, trans_b):
    h, m, _ = a.shape
    n = b.shape[1] if trans_b else b.shape[2]
    return pl.pallas_call(
        functools.partial(_wprod_kernel, trans_b=trans_b),
        out_shape=jax.ShapeDtypeStruct((h, m, n), F32),
        grid=(h,),
        in_specs=[pl.BlockSpec((None,) + a.shape[1:], lambda i: (i, 0, 0)),
                  pl.BlockSpec((None,) + b.shape[1:], lambda i: (i, 0, 0))],
        out_specs=pl.BlockSpec((None, m, n), lambda i: (i, 0, 0)),
        compiler_params=_params(("arbitrary",), 16 << 20),
        name="weight_product",
    )(a, b)


def _ab_in_kernel(x_ref, cos_ref, sin_ref, w_ref, wq_ref, qn_ref, kvn_ref,
                  c_ref, kr_ref, kcat_ref, q_ref, dq_ref, dk_ref, dv_ref, dkb_ref, dvb_ref, kn2_ref, kn2d_ref,
                  *, ql, kl, dw, rope, n_mla, n_diff, dq_scale):
    xb = x_ref[...].astype(BF16)
    cos = cos_ref[...]
    sin = sin_ref[...]

    def mm(lo, width):
        return _nn(xb, w_ref[:, lo:lo + width])

    o_ckv, o_dq, o_dk, o_dv, o_kr = ql, ql + kl, ql + kl + dw, ql + kl + 2 * dw, ql + kl + 3 * dw
    c = _rms(mm(o_ckv, kl)) * kvn_ref[...]
    krr = mm(o_kr, 2 * LANES)
    kr = krr[:, :LANES] * cos + krr[:, LANES:] * sin
    c_ref[...] = c
    kr_ref[...] = kr[:, :rope]
    cb = c.astype(BF16)
    krb = kr.astype(BF16)
    kcat_ref[:, :kl] = cb
    kcat_ref[:, kl:] = krb
    cf = cb.astype(F32)
    krf = krb.astype(F32)
    n2 = jnp.sum(cf * cf, axis=1, keepdims=True) + jnp.sum(krf * krf, axis=1, keepdims=True)
    tile_max = jnp.broadcast_to(jnp.max(n2, axis=0, keepdims=True), kn2_ref.shape)
    dq = mm(o_dq, dw) * dq_scale
    dk = mm(o_dk, dw)
    dv = mm(o_dv, dw)
    dk_ref[...] = dk
    dv_ref[...] = dv
    n2d = None
    for h in range(n_diff):
        sl = slice(h * LANES, (h + 1) * LANES)
        dq_ref[h] = dq[:, sl].astype(BF16)
        dkh = dk[:, sl].astype(BF16)
        dkb_ref[h] = dkh
        dvb_ref[h] = dv[:, sl].astype(BF16)
        dkf = dkh.astype(F32)
        n2h = jnp.sum(dkf * dkf, axis=1, keepdims=True)
        n2d = n2h if n2d is None else jnp.maximum(n2d, n2h)
    tile_max_d = jnp.broadcast_to(jnp.max(n2d, axis=0, keepdims=True), kn2d_ref.shape)

    @pl.when(pl.program_id(1) == 0)
    def _first_tile():
        kn2_ref[...] = tile_max
        kn2d_ref[...] = tile_max_d

    @pl.when(pl.program_id(1) > 0)
    def _later_tiles():
        kn2_ref[...] = jnp.maximum(kn2_ref[...], tile_max)
        kn2d_ref[...] = jnp.maximum(kn2d_ref[...], tile_max_d)
    cq = (_rms(mm(0, ql)) * qn_ref[...]).astype(BF16)
    hw = kl + 2 * LANES
    for h in range(n_mla):
        qh = _nn(cq, wq_ref[:, h * hw:(h + 1) * hw])
        q_ref[h, :, :kl] = qh[:, :kl].astype(BF16)
        q_ref[h, :, kl:] = (qh[:, kl:kl + LANES] * cos + qh[:, kl + LANES:] * sin).astype(BF16)


def _ab_in(x, cos, sin, w_aug, wq_big, q_norm, kv_norm, *, tm, dims):
    bx, tx, d = x.shape
    ql, kl, dw, rope, n_mla, n_diff = dims["ql"], dims["kl"], dims["dw"], dims["rope"], dims["n_mla"], dims["n_diff"]
    row = lambda b, i: (b, i, 0)
    hrow = lambda b, i: (b, 0, i, 0)
    tab = lambda b, i: (i, 0)
    out_shape = (
        jax.ShapeDtypeStruct((bx, tx, kl), F32),
        jax.ShapeDtypeStruct((bx, tx, rope), F32),
        jax.ShapeDtypeStruct((bx, tx, kl + LANES), BF16),
        jax.ShapeDtypeStruct((bx, n_mla, tx, kl + LANES), BF16),
        jax.ShapeDtypeStruct((bx, n_diff, tx, LANES), BF16),
        jax.ShapeDtypeStruct((bx, tx, dw), F32),
        jax.ShapeDtypeStruct((bx, tx, dw), F32),
        jax.ShapeDtypeStruct((bx, n_diff, tx, LANES), BF16),
        jax.ShapeDtypeStruct((bx, n_diff, tx, LANES), BF16),
        jax.ShapeDtypeStruct((bx, 8, LANES), F32),
        jax.ShapeDtypeStruct((bx, 8, LANES), F32),
    )
    out_specs = (
        pl.BlockSpec((None, tm, kl), row),
        pl.BlockSpec((None, tm, rope), row),
        pl.BlockSpec((None, tm, kl + LANES), row),
        pl.BlockSpec((None, n_mla, tm, kl + LANES), hrow),
        pl.BlockSpec((None, n_diff, tm, LANES), hrow),
        pl.BlockSpec((None, tm, dw), row),
        pl.BlockSpec((None, tm, dw), row),
        pl.BlockSpec((None, n_diff, tm, LANES), hrow),
        pl.BlockSpec((None, n_diff, tm, LANES), hrow),
        pl.BlockSpec((None, 8, LANES), lambda b, i: (b, 0, 0)),
        pl.BlockSpec((None, 8, LANES), lambda b, i: (b, 0, 0)),
    )
    nbytes = (2 * tm * d * 4 + w_aug.size * 2 + wq_big.size * 2
              + 2 * tm * (kl * 4 + LANES * 4 + (kl + LANES) * 2 + n_mla * (kl + LANES) * 2
                          + 3 * n_diff * LANES * 2 + 2 * dw * 4)
              + 6 * tm * dw * 4 + (8 << 20))
    return pl.pallas_call(
        functools.partial(_ab_in_kernel, ql=ql, kl=kl, dw=dw, rope=rope, n_mla=n_mla, n_diff=n_diff,
                          dq_scale=dims["dq_scale"]),
        out_shape=out_shape,
        grid=(bx, tx // tm),
        in_specs=[pl.BlockSpec((None, tm, d), row),
                  pl.BlockSpec((tm, LANES), tab),
                  pl.BlockSpec((tm, LANES), tab),
                  _const_spec(w_aug), _const_spec(wq_big), _const_spec(q_norm), _const_spec(kv_norm)],
        out_specs=out_specs,
        compiler_params=_params(("parallel", "arbitrary"), nbytes),
        name="ab_in_proj",
    )(x, cos, sin, w_aug, wq_big, q_norm, kv_norm)


def _tri_schedule(nq):
    qi = [i for i in range(nq) for _ in range(i + 1)]
    kj = [j for i in range(nq) for j in range(i + 1)]
    return jnp.asarray(qi, jnp.int32), jnp.asarray(kj, jnp.int32)


def _mla_prompt_kernel(qi_ref, kj_ref, q_ref, k_ref, km_ref, kn2_ref, o_ref, m_sc, l_sc, acc_sc, safe_sc,
                       *, n_heads, kl, n_meta, tq):
    step = pl.program_id(1)
    qi = qi_ref[step]
    kj = kj_ref[step]

    @pl.when(kj == 0)
    def _init():
        km = km_ref[...]
        vm = km_ref[:, :kl]
        valid = lax.broadcasted_iota(jnp.int32, (tq, MB), 1) < n_meta
        kn = jnp.sqrt(jnp.max(jnp.max(kn2_ref[...], axis=1, keepdims=True), axis=0, keepdims=True)) * NORM_SLACK

        def body(i, worst):
            for u in range(HEAD_UNROLL):
                h = i * HEAD_UNROLL + u
                q = q_ref[h]
                s = jnp.where(valid, _nt(q, km), NEG_INF)
                p = _softmax_first(s, m_sc, l_sc, h)
                acc_sc[h] = _nn(p.astype(BF16), vm)
                qf = q.astype(F32)
                qn = jnp.sqrt(jnp.sum(qf * qf, axis=1, keepdims=True)) * NORM_SLACK
                worst = jnp.maximum(worst, jnp.max(qn * kn - jnp.max(s, axis=1, keepdims=True)))
            return worst

        worst = lax.fori_loop(0, n_heads // HEAD_UNROLL, body, jnp.float32(-3.0e38))
        safe_sc[0] = (worst < EXP2_HEADROOM).astype(jnp.int32)

    def block(visible):
        k = k_ref[...]
        v = k_ref[:, :kl]

        def body(i, carry):
            for u in range(HEAD_UNROLL):
                h = i * HEAD_UNROLL + u
                s = _nt(q_ref[h], k)
                if visible is not None:
                    s = jnp.where(visible, s, NEG_INF)
                alpha, p = _softmax_next(s, m_sc, l_sc, h)
                acc_sc[h] = _lane_tile(alpha, kl) * acc_sc[h] + _nn(p.astype(BF16), v)
            return carry

        lax.fori_loop(0, n_heads // HEAD_UNROLL, body, 0)

    def block_shift_by_previous_max():
        k = k_ref[...]
        v = k_ref[:, :kl]

        def body(i, carry):
            for u in range(HEAD_UNROLL):
                h = i * HEAD_UNROLL + u
                s = _nt(q_ref[h], k)
                m_prev = m_sc[h]
                p = jnp.exp2(s - _lane_tile(m_prev, tq))
                m_new = jnp.maximum(m_prev, jnp.max(s, axis=1, keepdims=True))
                alpha = jnp.exp2(m_prev - m_new)
                l_sc[h] = alpha * (l_sc[h] + _lane_fold(p))
                acc_sc[h] = _lane_tile(alpha, kl) * (acc_sc[h] + _nn(p.astype(BF16), v))
                m_sc[h] = m_new
            return carry

        lax.fori_loop(0, n_heads // HEAD_UNROLL, body, 0)

    @pl.when(kj < qi)
    def _below_diagonal():
        lax.cond(safe_sc[0] == 1, block_shift_by_previous_max, lambda: block(None))

    @pl.when(kj == qi)
    def _diagonal():
        row = lax.broadcasted_iota(jnp.int32, (tq, tq), 0)
        col = lax.broadcasted_iota(jnp.int32, (tq, tq), 1)
        block(col <= row)
        for h in range(n_heads):
            o_ref[:, h * kl:(h + 1) * kl] = (acc_sc[h] / _softmax_total(l_sc, h)).astype(o_ref.dtype)


def _mla_meta_kernel(q_ref, km_ref, prev_ref, o_ref, *, n_heads, kl, n_meta):
    del prev_ref
    km = km_ref[...]
    vm = km_ref[:, :kl]
    row = lax.broadcasted_iota(jnp.int32, (MB, MB), 0)
    col = lax.broadcasted_iota(jnp.int32, (MB, MB), 1)
    visible = (col <= row) & (col < n_meta)
    for h in range(n_heads):
        s = jnp.where(visible, _nt(q_ref[h], km), NEG_INF)
        p = jnp.exp2(s - jnp.max(s, axis=1, keepdims=True))
        o = _nn(p.astype(BF16), vm) / jnp.sum(p, axis=1, keepdims=True)
        o_ref[:, h * kl:(h + 1) * kl] = o.astype(o_ref.dtype)


def _mla_prompt(q, kcat, kn2, *, t_real, n_meta, tq):
    b, n_heads, tp, kw = q.shape
    kl = kw - LANES
    nq = t_real // tq
    qi, kj = _tri_schedule(nq)
    meta_blk = t_real // MB
    grid_spec = pltpu.PrefetchScalarGridSpec(
        num_scalar_prefetch=2,
        grid=(b, int(qi.shape[0])),
        in_specs=[pl.BlockSpec((None, n_heads, tq, kw), lambda bb, s, qi_r, kj_r: (bb, 0, qi_r[s], 0)),
                  pl.BlockSpec((None, tq, kw), lambda bb, s, qi_r, kj_r: (bb, kj_r[s], 0)),
                  pl.BlockSpec((None, MB, kw), lambda bb, s, qi_r, kj_r: (bb, meta_blk, 0)),
                  pl.BlockSpec((None,) + kn2.shape[1:], lambda bb, s, qi_r, kj_r: (bb, 0, 0))],
        out_specs=pl.BlockSpec((None, tq, n_heads * kl), lambda bb, s, qi_r, kj_r: (bb, qi_r[s], 0)),
        scratch_shapes=[pltpu.VMEM((n_heads, tq, LANES), F32), pltpu.VMEM((n_heads, tq, LANES), F32),
                        pltpu.VMEM((n_heads, tq, kl), F32), pltpu.SMEM((1,), jnp.int32)],
    )
    nbytes = (2 * n_heads * tq * kw * 2 + 2 * tq * kw * 2 + 2 * tq * n_heads * kl * 2
              + n_heads * tq * (kl + 2 * LANES) * 4 + 8 * tq * tq * 4 + (8 << 20))
    o_real = pl.pallas_call(
        functools.partial(_mla_prompt_kernel, n_heads=n_heads, kl=kl, n_meta=n_meta, tq=tq),
        out_shape=jax.ShapeDtypeStruct((b, tp, n_heads * kl), BF16),
        grid_spec=grid_spec,
        compiler_params=_params(("parallel", "arbitrary"), nbytes),
        name="mla_prompt_attention",
    )(qi, kj, q, kcat, kcat, kn2)
    return pl.pallas_call(
        functools.partial(_mla_meta_kernel, n_heads=n_heads, kl=kl, n_meta=n_meta),
        out_shape=jax.ShapeDtypeStruct((b, tp, n_heads * kl), BF16),
        grid=(b,),
        in_specs=[pl.BlockSpec((None, n_heads, MB, kw), lambda bb: (bb, 0, meta_blk, 0)),
                  pl.BlockSpec((None, MB, kw), lambda bb: (bb, meta_blk, 0)),
                  pl.BlockSpec(memory_space=pl.ANY)],
        out_specs=pl.BlockSpec((None, MB, n_heads * kl), lambda bb: (bb, meta_blk, 0)),
        input_output_aliases={2: 0},
        compiler_params=_params(("parallel",), 16 << 20),
        name="mla_meta_attention",
    )(q, kcat, o_real)


def _lambda_diff(lam_ref, lambda_init):
    lam = lam_ref[...]
    return (jnp.exp(jnp.sum(lam[0:1] * lam[1:2], axis=1, keepdims=True))
            - jnp.exp(jnp.sum(lam[2:3] * lam[3:4], axis=1, keepdims=True)) + lambda_init)


def _diff_split(q, qk):
    lane = lax.broadcasted_iota(jnp.int32, q.shape, 1)
    zero = jnp.zeros_like(q)
    return jnp.concatenate([jnp.where(lane < qk, q, zero), jnp.where(lane >= qk, q, zero)], axis=0)


def _diff_prompt_kernel(qi_ref, kj_ref, q_ref, k_ref, v_ref, km_ref, vm_ref, lam_ref, sub_ref, kn2_ref,
                        o_ref, m_sc, l_sc, acc_sc, safe_sc, *, n_heads, qk, n_meta, tq, lambda_init):
    step = pl.program_id(1)
    qi = qi_ref[step]
    kj = kj_ref[step]

    @pl.when(kj == 0)
    def _init():
        valid = lax.broadcasted_iota(jnp.int32, (2 * tq, MB), 1) < n_meta
        kn = jnp.sqrt(jnp.max(jnp.max(kn2_ref[...], axis=1, keepdims=True), axis=0, keepdims=True)) * NORM_SLACK

        def body(i, worst):
            for u in range(HEAD_UNROLL):
                h = i * HEAD_UNROLL + u
                q2 = _diff_split(q_ref[h], qk)
                s = jnp.where(valid, _nt(q2, km_ref[h]), NEG_INF)
                p = _softmax_first(s, m_sc, l_sc, h)
                acc_sc[h] = _nn(p.astype(BF16), vm_ref[h])
                qf = q2.astype(F32)
                qn = jnp.sqrt(jnp.sum(qf * qf, axis=1, keepdims=True)) * NORM_SLACK
                worst = jnp.maximum(worst, jnp.max(qn * kn - jnp.max(s, axis=1, keepdims=True)))
            return worst

        worst = lax.fori_loop(0, n_heads // HEAD_UNROLL, body, jnp.float32(-3.0e38))
        safe_sc[0] = (worst < EXP2_HEADROOM).astype(jnp.int32)

    def block(visible):
        def body(i, carry):
            for u in range(HEAD_UNROLL):
                h = i * HEAD_UNROLL + u
                s = _nt(_diff_split(q_ref[h], qk), k_ref[h])
                if visible is not None:
                    s = jnp.where(visible, s, NEG_INF)
                alpha, p = _softmax_next(s, m_sc, l_sc, h)
                acc_sc[h] = alpha * acc_sc[h] + _nn(p.astype(BF16), v_ref[h])
            return carry

        lax.fori_loop(0, n_heads // HEAD_UNROLL, body, 0)

    def block_shift_by_previous_max():
        def body(i, carry):
            for u in range(HEAD_UNROLL):
                h = i * HEAD_UNROLL + u
                s = _nt(_diff_split(q_ref[h], qk), k_ref[h])
                m_prev = m_sc[h]
                p = jnp.exp2(s - _lane_tile(m_prev, tq))
                m_new = jnp.maximum(m_prev, jnp.max(s, axis=1, keepdims=True))
                alpha = jnp.exp2(m_prev - m_new)
                l_sc[h] = alpha * (l_sc[h] + _lane_fold(p))
                acc_sc[h] = alpha * (acc_sc[h] + _nn(p.astype(BF16), v_ref[h]))
                m_sc[h] = m_new
            return carry

        lax.fori_loop(0, n_heads // HEAD_UNROLL, body, 0)

    @pl.when(kj < qi)
    def _below_diagonal():
        lax.cond(safe_sc[0] == 1, block_shift_by_previous_max, lambda: block(None))

    @pl.when(kj == qi)
    def _diagonal():
        row = lax.broadcasted_iota(jnp.int32, (2 * tq, tq), 0)
        row = jnp.where(row >= tq, row - tq, row)
        col = lax.broadcasted_iota(jnp.int32, (2 * tq, tq), 1)
        block(col <= row)
        lam = _lambda_diff(lam_ref, lambda_init)
        for h in range(n_heads):
            o = acc_sc[h] / _softmax_total(l_sc, h)
            o = o[:tq] - lam * o[tq:]
            o = _rms(o) * sub_ref[...] * (1.0 - lambda_init)
            o_ref[:, h * LANES:(h + 1) * LANES] = o.astype(o_ref.dtype)


def _diff_meta_kernel(q_ref, km_ref, vm_ref, lam_ref, sub_ref, prev_ref, o_ref,
                      *, n_heads, qk, n_meta, lambda_init):
    del prev_ref
    row = lax.broadcasted_iota(jnp.int32, (2 * MB, MB), 0)
    row = jnp.where(row >= MB, row - MB, row)
    col = lax.broadcasted_iota(jnp.int32, (2 * MB, MB), 1)
    visible = (col <= row) & (col < n_meta)
    lam = _lambda_diff(lam_ref, lambda_init)
    for h in range(n_heads):
        s = jnp.where(visible, _nt(_diff_split(q_ref[h], qk), km_ref[h]), NEG_INF)
        p = jnp.exp2(s - jnp.max(s, axis=1, keepdims=True))
        o = _nn(p.astype(BF16), vm_ref[h]) / jnp.sum(p, axis=1, keepdims=True)
        o = o[:MB] - lam * o[MB:]
        o = _rms(o) * sub_ref[...] * (1.0 - lambda_init)
        o_ref[:, h * LANES:(h + 1) * LANES] = o.astype(o_ref.dtype)


def _diff_prompt(q, k, v, lam, subln, kn2, *, t_real, n_meta, tq, qk, lambda_init):
    b, n_heads, tp, _ = q.shape
    nq = t_real // tq
    qi, kj = _tri_schedule(nq)
    meta_blk = t_real // MB
    qmap = lambda bb, s, qi_r, kj_r: (bb, 0, qi_r[s], 0)
    kmap = lambda bb, s, qi_r, kj_r: (bb, 0, kj_r[s], 0)
    mmap = lambda bb, s, qi_r, kj_r: (bb, 0, meta_blk, 0)
    cmap = lambda bb, s, qi_r, kj_r: (0, 0)
    grid_spec = pltpu.PrefetchScalarGridSpec(
        num_scalar_prefetch=2,
        grid=(b, int(qi.shape[0])),
        in_specs=[pl.BlockSpec((None, n_heads, tq, LANES), qmap),
                  pl.BlockSpec((None, n_heads, tq, LANES), kmap),
                  pl.BlockSpec((None, n_heads, tq, LANES), kmap),
                  pl.BlockSpec((None, n_heads, MB, LANES), mmap),
                  pl.BlockSpec((None, n_heads, MB, LANES), mmap),
                  pl.BlockSpec(lam.shape, cmap),
                  pl.BlockSpec(subln.shape, cmap),
                  pl.BlockSpec((None,) + kn2.shape[1:], lambda bb, s, qi_r, kj_r: (bb, 0, 0))],
        out_specs=pl.BlockSpec((None, tq, n_heads * LANES), lambda bb, s, qi_r, kj_r: (bb, qi_r[s], 0)),
        scratch_shapes=[pltpu.VMEM((n_heads, 2 * tq, LANES), F32), pltpu.VMEM((n_heads, 2 * tq, LANES), F32),
                        pltpu.VMEM((n_heads, 2 * tq, LANES), F32), pltpu.SMEM((1,), jnp.int32)],
    )
    nbytes = (6 * n_heads * tq * LANES * 2 + 2 * tq * n_heads * LANES * 2
              + n_heads * 2 * tq * 3 * LANES * 4 + 8 * 2 * tq * tq * 4 + (8 << 20))
    o_real = pl.pallas_call(
        functools.partial(_diff_prompt_kernel, n_heads=n_heads, qk=qk, n_meta=n_meta, tq=tq,
                          lambda_init=lambda_init),
        out_shape=jax.ShapeDtypeStruct((b, tp, n_heads * LANES), BF16),
        grid_spec=grid_spec,
        compiler_params=_params(("parallel", "arbitrary"), nbytes),
        name="diff_prompt_attention",
    )(qi, kj, q, k, v, k, v, lam, subln, kn2)
    mmap1 = lambda bb: (bb, 0, meta_blk, 0)
    return pl.pallas_call(
        functools.partial(_diff_meta_kernel, n_heads=n_heads, qk=qk, n_meta=n_meta, lambda_init=lambda_init),
        out_shape=jax.ShapeDtypeStruct((b, tp, n_heads * LANES), BF16),
        grid=(b,),
        in_specs=[pl.BlockSpec((None, n_heads, MB, LANES), mmap1),
                  pl.BlockSpec((None, n_heads, MB, LANES), mmap1),
                  pl.BlockSpec((None, n_heads, MB, LANES), mmap1),
                  pl.BlockSpec(lam.shape, lambda bb: (0, 0)),
                  pl.BlockSpec(subln.shape, lambda bb: (0, 0)),
                  pl.BlockSpec(memory_space=pl.ANY)],
        out_specs=pl.BlockSpec((None, MB, n_heads * LANES), lambda bb: (bb, meta_blk, 0)),
        input_output_aliases={5: 0},
        compiler_params=_params(("parallel",), 16 << 20),
        name="diff_meta_attention",
    )(q, k, v, lam, subln, o_real)


def _new_token_mask(rows, page, t_new):
    t = lax.broadcasted_iota(jnp.int32, (rows, page), 0) % t_new
    col = lax.broadcasted_iota(jnp.int32, (rows, page), 1)
    return col <= t


def _mla_pages_step(q_ref, cn_ref, krn_ref, lat_refs, rope_refs, o_ref, m_sc, l_sc, acc_sc, *, kl, rope, t_new):
    j = pl.program_id(1)
    q_lat = q_ref[:, :kl]
    q_pe = q_ref[:, kl:kl + rope]
    rows = q_ref.shape[0]
    page = cn_ref.shape[0]

    @pl.when(j == 0)
    def _init():
        cn = cn_ref[...].astype(BF16)
        s = _nt(q_lat, cn) + _nt(q_pe, krn_ref[...].astype(BF16))
        p = _softmax_first(jnp.where(_new_token_mask(rows, page, t_new), s, NEG_INF), m_sc, l_sc, 0)
        acc_sc[...] = _nn(p.astype(BF16), cn)

    lat = jnp.concatenate([r[...].astype(BF16) for r in lat_refs], axis=0)
    kr_t = jnp.concatenate([r[...].astype(BF16) for r in rope_refs], axis=1)
    alpha, p = _softmax_next(_nt(q_lat, lat) + _nn(q_pe, kr_t), m_sc, l_sc, 0)
    acc_sc[...] = _lane_tile(alpha, kl) * acc_sc[...] + _nn(p.astype(BF16), lat)

    @pl.when(j == pl.num_programs(1) - 1)
    def _finish():
        o_ref[...] = (acc_sc[...] / _softmax_total(l_sc, 0)).astype(o_ref.dtype)


def _page_specs(pool, li, n_pages, n_tab):
    blk = (None, None) + pool.shape[2:]
    specs = []
    for g in range(n_pages):
        specs.append(pl.BlockSpec(
            blk, lambda b, j, pt, g=g: (li, pt[b * n_tab + j * n_pages + g]) + (0,) * (pool.ndim - 2)))
    return specs


def _diff_pages_step(q_ref, kn_ref, vn_ref, lam_ref, sub_ref, k_refs, v_refs, o_ref, m_sc, l_sc, acc_sc,
                     *, n_heads, t_new, lambda_init):
    j = pl.program_id(1)
    q = q_ref[...]
    rows, width = q.shape
    page = kn_ref.shape[0]

    @pl.when(j == 0)
    def _init():
        s = _nt(q, kn_ref[...].astype(BF16))
        p = _softmax_first(jnp.where(_new_token_mask(rows, page, t_new), s, NEG_INF), m_sc, l_sc, 0)
        acc_sc[...] = _nn(p.astype(BF16), vn_ref[...].astype(BF16))

    k_t = jnp.concatenate([r[...].astype(BF16) for r in k_refs], axis=1)
    alpha, p = _softmax_next(_nn(q, k_t), m_sc, l_sc, 0)
    v = jnp.concatenate(
        [jnp.concatenate([r[pl.ds(h, page, stride=n_heads), :].astype(BF16) for h in range(n_heads)], axis=1)
         for r in v_refs], axis=0)
    acc_sc[...] = _lane_tile(alpha, width) * acc_sc[...] + _nn(p.astype(BF16), v)

    @pl.when(j == pl.num_programs(1) - 1)
    def _finish():
        lam = _lambda_diff(lam_ref, lambda_init)
        o = acc_sc[...] / _softmax_total(l_sc, 0)
        per_head = rows // n_heads
        for h in range(n_heads):
            blk = o[h * per_head:(h + 1) * per_head, h * LANES:(h + 1) * LANES]
            d = blk - lam * pltpu.roll(blk, per_head // 2, 0)
            d = _rms(d) * sub_ref[...] * (1.0 - lambda_init)
            o_ref[h * per_head:(h + 1) * per_head, :] = d.astype(o_ref.dtype)


def _paged_decode_kernel(pt_ref, qm_ref, cn_ref, krn_ref, qd_ref, kn_ref, vn_ref, lam_ref, sub_ref, *rest,
                         n_pages, kl, rope, n_heads, t_new, lambda_init):
    del pt_ref
    g = n_pages
    lat_refs, rope_refs, k_refs, v_refs = rest[:g], rest[g:2 * g], rest[2 * g:3 * g], rest[3 * g:4 * g]
    om_ref, od_ref, mm_sc, lm_sc, am_sc, md_sc, ld_sc, ad_sc = rest[4 * g:]
    _mla_pages_step(qm_ref, cn_ref, krn_ref, lat_refs, rope_refs, om_ref, mm_sc, lm_sc, am_sc,
                    kl=kl, rope=rope, t_new=t_new)
    _diff_pages_step(qd_ref, kn_ref, vn_ref, lam_ref, sub_ref, k_refs, v_refs, od_ref, md_sc, ld_sc, ad_sc,
                     n_heads=n_heads, t_new=t_new, lambda_init=lambda_init)


def _paged_decode(q_mla, c_new, kr_new, lat_pool, rope_pool_t, q_diff, k_new, v_new, lam, subln, k_pool_t, v_pool,
                  page_table, li, *, t_new, n_heads, pages_per_step, lambda_init):
    n_seq, rows, kw = q_mla.shape
    kl = kw - LANES
    rope, page = rope_pool_t.shape[2], rope_pool_t.shape[3]
    rows_d, width = q_diff.shape[1], q_diff.shape[2]
    n_tab = page_table.shape[1]
    g = pages_per_step
    seq3 = lambda b, j, pt: (b, 0, 0)
    cst = lambda b, j, pt: (0, 0)
    grid_spec = pltpu.PrefetchScalarGridSpec(
        num_scalar_prefetch=1,
        grid=(n_seq, n_tab // g),
        in_specs=[pl.BlockSpec((None, rows, kw), seq3),
                  pl.BlockSpec((None, page, kl), seq3),
                  pl.BlockSpec((None, page, rope), seq3),
                  pl.BlockSpec((None, rows_d, width), seq3),
                  pl.BlockSpec((None, page, width), seq3),
                  pl.BlockSpec((None, page, width), seq3),
                  pl.BlockSpec(lam.shape, cst),
                  pl.BlockSpec(subln.shape, cst)]
                 + _page_specs(lat_pool, li, g, n_tab) + _page_specs(rope_pool_t, li, g, n_tab)
                 + _page_specs(k_pool_t, li, g, n_tab) + _page_specs(v_pool, li, g, n_tab),
        out_specs=[pl.BlockSpec((None, rows, kl), seq3), pl.BlockSpec((None, rows_d, LANES), seq3)],
        scratch_shapes=[pltpu.VMEM((1, rows, LANES), F32), pltpu.VMEM((1, rows, LANES), F32),
                        pltpu.VMEM((rows, kl), F32),
                        pltpu.VMEM((1, rows_d, LANES), F32), pltpu.VMEM((1, rows_d, LANES), F32),
                        pltpu.VMEM((rows_d, width), F32)],
    )
    page_bytes = page * (kl + rope + 2 * width)
    nbytes = 2 * g * page_bytes * 4 + 2 * g * page_bytes * 2 + (8 << 20)
    return pl.pallas_call(
        functools.partial(_paged_decode_kernel, n_pages=g, kl=kl, rope=rope, n_heads=n_heads, t_new=t_new,
                          lambda_init=lambda_init),
        out_shape=[jax.ShapeDtypeStruct((n_seq, rows, kl), BF16), jax.ShapeDtypeStruct((n_seq, rows_d, LANES), F32)],
        grid_spec=grid_spec,
        compiler_params=_params(("parallel", "arbitrary"), nbytes),
        name="paged_attention",
    )(page_table.reshape(-1), q_mla, c_new, kr_new, q_diff, k_new, v_new, lam, subln,
      *([lat_pool] * g), *([rope_pool_t] * g), *([k_pool_t] * g), *([v_pool] * g))


def _post_kernel(*refs, n_act, alpha, tf):
    x_ref = refs[0]
    act_refs = refs[1:1 + n_act]
    w_refs = refs[1 + n_act:1 + 2 * n_act]
    g1_ref, b1_ref, g2_ref, b2_ref, w1_ref, w2_ref, o_ref = refs[1 + 2 * n_act:]
    mix = _nn(act_refs[0][...], w_refs[0][...])
    for a_ref, w_ref in zip(act_refs[1:], w_refs[1:]):
        mix = mix + _nn(a_ref[...], w_ref[...])
    x1 = _layer_norm(alpha * x_ref[...] + mix, g1_ref[...], b1_ref[...])
    x1b = x1.astype(BF16)
    d_ff = w1_ref.shape[1]
    acc = None
    for c in range(d_ff // tf):
        hid = jnp.maximum(_nn(x1b, w1_ref[:, c * tf:(c + 1) * tf]), 0.0)
        part = _nn((hid * hid).astype(BF16), w2_ref[c * tf:(c + 1) * tf, :])
        acc = part if acc is None else acc + part
    o_ref[...] = _layer_norm(alpha * x1 + acc, g2_ref[...], b2_ref[...])


def _post(x, acts, ws, g1, b1, g2, b2, w1, w2, *, tm, alpha, t_out=None):
    bx, tx, d = x.shape
    t_out = tx if t_out is None else t_out
    row = lambda b, i: (b, i, 0)
    tf = min(512, w1.shape[1])
    nbytes = (4 * tm * d * 4 + sum(2 * tm * a.shape[2] * 2 for a in acts) + sum(w.size * 2 for w in ws)
              + (w1.size + w2.size) * 2 + 6 * tm * d * 4 + 4 * tm * tf * 4 + (8 << 20))
    return pl.pallas_call(
        functools.partial(_post_kernel, n_act=len(acts), alpha=alpha, tf=tf),
        out_shape=jax.ShapeDtypeStruct((bx, t_out, d), F32),
        grid=(bx, t_out // tm),
        in_specs=[pl.BlockSpec((None, tm, d), row)]
                 + [pl.BlockSpec((None, tm, a.shape[2]), row) for a in acts]
                 + [_const_spec(w) for w in ws]
                 + [_const_spec(v) for v in (g1, b1, g2, b2, w1, w2)],
        out_specs=pl.BlockSpec((None, tm, d), row),
        compiler_params=_params(("parallel", "parallel"), nbytes),
        name="out_proj_mlp",
    )(x, *acts, *ws, g1, b1, g2, b2, w1, w2)


def _log_sigmoid(x):
    return -(jnp.maximum(-x, 0.0) + jnp.log1p(jnp.exp(-jnp.abs(x))))


def _cd_in_kernel(x_ref, cos_ref, sin_ref, w_ref, gb_ref, nrm_ref,
                  rq_ref, rk_ref, rv_ref, rg_ref, mq_ref, mk_ref, mv_ref, mg_ref, gt_ref,
                  *, qw, vw, n_ml, rk_scale, mk_scale):
    xb = x_ref[...].astype(BF16)
    cos = cos_ref[...]
    sin = sin_ref[...]

    def mm(lo, width):
        return _nn(xb, w_ref[:, lo:lo + width])

    rq_ref[...] = mm(0, qw) * cos + mm(qw, qw) * sin
    rk_ref[...] = (mm(2 * qw, qw) * cos + mm(3 * qw, qw) * sin) * rk_scale
    o = 4 * qw
    rv_ref[...] = mm(o, vw).astype(BF16)
    rg = mm(o + vw, vw)
    rg_ref[...] = rg * jax.nn.sigmoid(rg)
    o = o + 2 * vw
    mq_ref[...] = mm(o, qw)
    mk_ref[...] = mm(o + qw, qw) * mk_scale
    o = o + 2 * qw
    mv_ref[...] = mm(o, vw).astype(BF16)
    mg_ref[...] = jax.nn.sigmoid(mm(o + vw, vw)) * nrm_ref[...]
    pre = mm(o + 2 * vw, LANES) + gb_ref[...]
    lane = lax.broadcasted_iota(jnp.int32, pre.shape, 1)
    gt_ref[...] = jnp.where(lane < n_ml, pre, _log_sigmoid(pre))


def _cd_in(x, cos, sin, w_aug, gate_bias, ml_norm, *, tm, dims):
    bx, tx, d = x.shape
    qw, vw = dims["cd_qw"], dims["cd_vw"]
    row = lambda b, i: (b, i, 0)
    tab = lambda b, i: (i, 0)
    shapes = [(qw, F32), (qw, F32), (vw, BF16), (vw, F32), (qw, F32), (qw, F32), (vw, BF16), (vw, F32), (LANES, F32)]
    nbytes = (2 * tm * d * 4 + w_aug.size * 2 + 2 * tm * sum(w * (4 if t == F32 else 2) for w, t in shapes)
              + 8 * tm * vw * 4 + (8 << 20))
    return pl.pallas_call(
        functools.partial(_cd_in_kernel, qw=qw, vw=vw, n_ml=dims["n_ml"],
                          rk_scale=dims["rk_scale"], mk_scale=dims["mk_scale"]),
        out_shape=tuple(jax.ShapeDtypeStruct((bx, tx, w), t) for w, t in shapes),
        grid=(bx, tx // tm),
        in_specs=[pl.BlockSpec((None, tm, d), row),
                  pl.BlockSpec((tm, qw), tab), pl.BlockSpec((tm, qw), tab),
                  _const_spec(w_aug), _const_spec(gate_bias), _const_spec(ml_norm)],
        out_specs=tuple(pl.BlockSpec((None, tm, w), row) for w, _ in shapes),
        compiler_params=_params(("parallel", "parallel"), nbytes),
        name="cd_in_proj",
    )(x, cos, sin, w_aug, gate_bias, ml_norm)


def _split_bf16(x):
    hi = x.astype(BF16)
    lo = (x - hi.astype(F32)).astype(BF16)
    return hi, lo


def _chunk_kernel(ids_ref, rq_ref, rk_ref, rv_ref, rg_ref, mq_ref, mk_ref, mv_ref, mg_ref, gt_ref,
                  dec_ref, lgr_ref, lgc_ref, s0_ref, c0_ref, n0_ref, m0_ref,
                  o_ref, s_out, c_out, n_out, m_out,
                  s_sc, c_sc, n_sc, m_sc,
                  *, n_heads, dk, dv, first_valid, n_seq_blk):
    del ids_ref
    ci = pl.program_id(1)

    @pl.when(ci == 0)
    def _load_state():
        s_sc[...] = s0_ref[...]
        c_sc[...] = c0_ref[...]
        n_sc[...] = n0_ref[...]
        m_sc[...] = m0_ref[...]

    n_valid = jnp.where(ci == 0, first_valid, MB)
    for bi in range(n_seq_blk):
        _chunk_step(bi, n_valid, rq_ref, rk_ref, rv_ref, rg_ref, mq_ref, mk_ref, mv_ref, mg_ref, gt_ref,
                    dec_ref, lgr_ref, lgc_ref, o_ref, s_sc, c_sc, n_sc, m_sc, n_heads=n_heads, dk=dk, dv=dv)

    @pl.when(ci == pl.num_programs(1) - 1)
    def _store_state():
        s_out[...] = s_sc[...]
        c_out[...] = c_sc[...]
        n_out[...] = n_sc[...]
        m_out[...] = m_sc[...]


def _chunk_step(bi, n_valid, rq_ref, rk_ref, rv_ref, rg_ref, mq_ref, mk_ref, mv_ref, mg_ref, gt_ref,
                dec_ref, lgr_ref, lgc_ref, o_ref, s_sc, c_sc, n_sc, m_sc, *, n_heads, dk, dv):
    qw = n_heads * dk
    idx_c = lax.broadcasted_iota(jnp.int32, (MB, 1), 0)
    live_c = idx_c < n_valid
    lane_q = lax.broadcasted_iota(jnp.int32, (MB, qw), 1)
    t_f = idx_c.astype(F32)
    nv_f = n_valid.astype(F32)

    lgr = lgr_ref[...]
    rq = rq_ref[bi]
    rk = jnp.where(live_c, rk_ref[bi], 0.0)
    rv = rv_ref[bi]
    q_dec = rq * jnp.exp(lgr * (t_f + 1.0))
    k_dec = (rk * jnp.exp(lgr * jnp.where(live_c, nv_f - 1.0 - t_f, 0.0))).astype(BF16)
    rk_b = rk.astype(BF16)
    s_all = s_sc[bi]
    s_b = s_all.astype(BF16)
    upd = _tn(k_dec, rv)
    s_dec = jnp.exp(lgc_ref[...] * nv_f)
    for h in range(n_heads):
        head = (lane_q >= h * dk) & (lane_q < (h + 1) * dk)
        inner = _nt(jnp.where(head, rq, 0.0).astype(BF16), rk_b) * dec_ref[h]
        o = _nn(inner.astype(BF16), rv[:, h * dv:(h + 1) * dv]) + _nn(jnp.where(head, q_dec, 0.0).astype(BF16), s_b)
        o = _head_norm(o) * rg_ref[bi, :, h * dv:(h + 1) * dv]
        o_ref[bi, :, h * dv:(h + 1) * dv] = o.astype(o_ref.dtype)
        s_sc[bi, h * dk:(h + 1) * dk, :] = (s_dec[h * dk:(h + 1) * dk] * s_all[h * dk:(h + 1) * dk]
                                            + upd[h * dk:(h + 1) * dk, h * dv:(h + 1) * dv])

    gt = gt_ref[bi]
    lane_g = lax.broadcasted_iota(jnp.int32, gt.shape, 1)
    gt = jnp.where(live_c, gt, jnp.where(lane_g < n_heads, NEG_INF, 0.0))
    row_i = lax.broadcasted_iota(jnp.int32, (MB, MB), 0)
    col_i = lax.broadcasted_iota(jnp.int32, (MB, MB), 1)
    causal = col_i <= row_i
    tri = causal.astype(BF16)
    g_hi, g_lo = _split_bf16(gt)
    cum_c = _nn(tri, g_hi) + _nn(tri, g_lo)
    gt_r = gt.T
    cum_r = cum_c.T
    mq = mq_ref[bi]
    mk = jnp.where(live_c, mk_ref[bi], 0.0)
    mk_b = mk.astype(BF16)
    mv = mv_ref[bi]
    c_cat = c_sc[bi]
    c_b = c_cat.astype(BF16)
    n_cat = n_sc[bi]
    m_all = m_sc[bi]
    c_new = jnp.zeros_like(c_cat)
    n_new = jnp.zeros_like(n_cat)
    g_keep = jnp.zeros_like(n_cat)
    m_new = jnp.zeros_like(m_all)
    lane_m = lax.broadcasted_iota(jnp.int32, m_all.shape, 1)
    lane_n = lax.broadcasted_iota(jnp.int32, n_cat.shape, 1)
    for h in range(n_heads):
        head = (lane_q >= h * dk) & (lane_q < (h + 1) * dk)
        head_n = (lane_n >= h * dk) & (lane_n < (h + 1) * dk)
        b_c = cum_c[:, n_heads + h:n_heads + h + 1]
        i_c = gt[:, h:h + 1]
        b_r = cum_r[n_heads + h:n_heads + h + 1, :]
        i_r = gt_r[h:h + 1, :]
        m_prev = m_all[:, h:h + 1]
        d = jnp.where(causal, b_c - b_r + i_r, NEG_INF)
        g = b_c + m_prev
        m_t = jnp.maximum(g, jnp.max(d, axis=1, keepdims=True))
        w = jnp.exp(d - m_t)
        gs = jnp.exp(g - m_t)
        qh = jnp.where(head, mq, 0.0)
        qh_b = qh.astype(BF16)
        a = _nt(qh_b, mk_b) * w
        num = _nn(a.astype(BF16), mv[:, h * dv:(h + 1) * dv]) + gs * _nt(qh_b, c_b)
        den = jnp.sum(a, axis=1, keepdims=True) + gs * jnp.sum(qh * n_cat, axis=1, keepdims=True)
        hid = num / jnp.maximum(jnp.abs(den), jnp.exp(-m_t))
        hid = _head_norm(hid) * mg_ref[bi, :, h * dv:(h + 1) * dv]
        o_ref[bi, :, n_heads * dv + h * dv:n_heads * dv + (h + 1) * dv] = hid.astype(o_ref.dtype)
        m_last = m_t[MB - 1:MB]
        w_last = jnp.exp(b_c[MB - 1:MB] - b_c + i_c - m_last)
        g_last = gs[MB - 1:MB]
        kh = jnp.where(head, mk, 0.0)
        c_new = c_new + _tn((mv[:, h * dv:(h + 1) * dv].astype(F32) * w_last).astype(BF16), kh.astype(BF16))
        n_new = n_new + jnp.sum(kh * w_last, axis=0, keepdims=True)
        g_keep = jnp.where(head_n, g_last, g_keep)
        m_new = jnp.where(lane_m == h, m_last, m_new)
    c_sc[bi] = g_keep * c_cat + c_new
    n_sc[bi] = g_keep * n_cat + n_new
    m_sc[bi] = m_new


def _chunk_scan(rq, rk, rv, rg, mq, mk, mv, mg, gt, dec, lgr, lgc, s0, c0, n0, m0,
                *, n_heads, dk, dv, blocks, first_valid, n_seq_blk):
    bx, tx, qw = rq.shape
    vw = rv.shape[2]
    nb = n_seq_blk
    assert bx % nb == 0
    blk_ids = jnp.asarray(blocks, jnp.int32)
    row = lambda b, c, ids: (b, ids[c], 0)
    st = lambda b, c, ids: (b, 0, 0)
    cst3 = lambda b, c, ids: (0, 0, 0)
    cst2 = lambda b, c, ids: (0, 0)
    state_shapes = [(qw, dv), (dv, qw), (1, qw), (1, LANES)]
    grid_spec = pltpu.PrefetchScalarGridSpec(
        num_scalar_prefetch=1,
        grid=(bx // nb, len(blocks)),
        in_specs=[pl.BlockSpec((nb, MB, qw), row), pl.BlockSpec((nb, MB, qw), row),
                  pl.BlockSpec((nb, MB, vw), row), pl.BlockSpec((nb, MB, vw), row),
                  pl.BlockSpec((nb, MB, qw), row), pl.BlockSpec((nb, MB, qw), row),
                  pl.BlockSpec((nb, MB, vw), row), pl.BlockSpec((nb, MB, vw), row),
                  pl.BlockSpec((nb, MB, LANES), row),
                  pl.BlockSpec(dec.shape, cst3), pl.BlockSpec(lgr.shape, cst2), pl.BlockSpec(lgc.shape, cst2)]
                 + [pl.BlockSpec((nb,) + s, st) for s in state_shapes],
        out_specs=[pl.BlockSpec((nb, MB, 2 * vw), row)] + [pl.BlockSpec((nb,) + s, st) for s in state_shapes],
        scratch_shapes=[pltpu.VMEM((nb,) + s, F32) for s in state_shapes],
    )
    out_shape = [jax.ShapeDtypeStruct((bx, tx, 2 * vw), BF16)] + [jax.ShapeDtypeStruct((bx,) + s, F32) for s in state_shapes]
    return pl.pallas_call(
        functools.partial(_chunk_kernel, n_heads=n_heads, dk=dk, dv=dv, first_valid=first_valid, n_seq_blk=nb),
        out_shape=out_shape,
        grid_spec=grid_spec,
        compiler_params=_params(("parallel", "arbitrary"), 32 << 20),
        name="retention_mlstm_chunks",
    )(blk_ids, rq, rk, rv, rg, mq, mk, mv, mg, gt, dec, lgr, lgc, s0, c0, n0, m0)


def _rope_tables(pos, half, reps, width):
    freqs = ROPE_BASE ** (-jnp.arange(half, dtype=F32) / half)
    ang = pos.astype(F32)[:, None] * freqs[None, :]
    cos, sin = jnp.cos(ang), jnp.sin(ang)
    cos = jnp.tile(jnp.concatenate([cos, cos], axis=1), (1, reps))
    sin = jnp.tile(jnp.concatenate([-sin, sin], axis=1), (1, reps))
    pad = width - cos.shape[1]
    return jnp.pad(cos, ((0, 0), (0, pad))), jnp.pad(sin, ((0, 0), (0, pad)))


def _swap_halves(w, group):
    shp = w.shape
    w = w.reshape(shp[:-1] + (shp[-1] // group, 2, group // 2))
    return w[..., ::-1, :].reshape(shp)


def _pad_cols(w, width):
    return jnp.pad(w, ((0, 0), (0, width - w.shape[1])))


def _pad_page(rows, n_seq, t_new, page):
    w = rows.shape[-1]
    return jnp.pad(rows.reshape(n_seq, t_new, w), ((0, 0), (0, page - t_new), (0, 0)))


def _meta_first(a, t_real, n_meta):
    return jnp.concatenate([a[:, t_real:t_real + n_meta], a[:, :t_real]], axis=1)


def kernel(x_prompt, x_sample, cache_mla_latent, cache_mla_rope, cache_diff_k, cache_diff_v, state_ret, state_mlstm_C, state_mlstm_n, state_mlstm_m, page_table, meta_tokens, ab_w_in, mla_q_norm, mla_w_uq, mla_kv_norm, mla_w_uk, mla_w_uv, diff_lambda_q1, diff_lambda_k1, diff_lambda_q2, diff_lambda_k2, diff_subln, ab_w_out, cd_w_in, ml_b_i, ml_b_f, ml_norm, cd_w_out, ln1_g, ln1_b, ln2_g, ln2_b, mlp_w1, mlp_w2):
    b, t_real, d = x_prompt.shape
    n_seq, t_new, _ = x_sample.shape
    n_meta = meta_tokens.shape[0]
    depth = ln1_g.shape[0]
    alpha = (2 * depth) ** 0.25
    page = cache_mla_latent.shape[2]
    past_len = page_table.shape[1] * page
    ql, kl, rope = mla_q_norm.shape[1], mla_kv_norm.shape[1], cache_mla_rope.shape[3]
    n_mla, nope, mla_v = mla_w_uk.shape[2], mla_w_uk.shape[3], mla_w_uv.shape[3]
    n_diff, diff_qk, diff_v = cache_diff_k.shape[3], cache_diff_k.shape[5], cache_diff_v.shape[4]
    n_ret, ret_dk, ret_dv = state_ret.shape[2], state_ret.shape[3], state_ret.shape[4]
    n_ml, ml_dv, ml_dk = state_mlstm_C.shape[2], state_mlstm_C.shape[3], state_mlstm_C.shape[4]
    assert 2 * diff_qk == LANES and diff_v == LANES and n_meta <= MB and t_real % MB == 0 and page == MB
    assert (n_ret, ret_dk, ret_dv) == (n_ml, ml_dk, ml_dv) and ret_dv == LANES and t_new <= MB
    dw = n_diff * LANES
    dims = dict(ql=ql, kl=kl, dw=dw, rope=rope, n_mla=n_mla, n_diff=n_diff, dq_scale=diff_qk ** -0.5 * LOG2E,
                cd_qw=n_ret * ret_dk, cd_vw=n_ret * ret_dv, n_ml=n_ml, rk_scale=ret_dk ** -0.5, mk_scale=ml_dk ** -0.5)
    tp = t_real + MB
    n_rows_s = n_seq * t_new
    tm_p = _tile(tp, 640)
    tm_s = _tile(n_rows_s, 512)
    tq = _tile(t_real, 512)

    xp = jnp.concatenate([x_prompt, jnp.broadcast_to(meta_tokens.astype(x_prompt.dtype)[None], (b, n_meta, d)),
                          jnp.zeros((b, MB - n_meta, d), x_prompt.dtype)], axis=1)
    xs = x_sample.reshape(1, n_rows_s, d)
    pos_p = jnp.concatenate([n_meta + jnp.arange(t_real), jnp.arange(MB)])
    pos_s = past_len + (jnp.arange(n_rows_s) % t_new)

    prompt_out, sample_out = {}, {}
    for layer in range(depth):
        g1, b1, g2, b2 = (v[layer][None] for v in (ln1_g, ln1_b, ln2_g, ln2_b))
        w1, w2 = mlp_w1[layer].astype(BF16), mlp_w2[layer].astype(BF16)
        last = layer == depth - 1
        if layer % 2 == 0:
            li = layer // 2
            lambda_init = 0.8 - 0.6 * math.exp(-0.3 * layer)
            scale = (nope + rope) ** -0.5 * LOG2E
            w_in = ab_w_in[li]
            o = [0, ql, ql + kl, ql + kl + rope, ql + kl + rope + dw, ql + kl + rope + 2 * dw, ql + kl + rope + 3 * dw]
            w_cq, w_ckv, w_kr, w_dq, w_dk, w_dv = (w_in[:, o[i]:o[i + 1]] for i in range(6))
            w_aug = jnp.concatenate([w_cq, w_ckv, w_dq, w_dk, w_dv, _pad_cols(w_kr, LANES),
                                     _pad_cols(_swap_halves(w_kr, rope), LANES)], axis=1).astype(BF16)
            w_uq = mla_w_uq[li].reshape(ql, n_mla, nope + rope)
            w_abs = _wprod(w_uq[:, :, :nope].transpose(1, 0, 2), mla_w_uk[li].transpose(1, 0, 2), True)
            w_pe = w_uq[:, :, nope:].transpose(1, 0, 2)
            pe_pad = ((0, 0), (0, 0), (0, LANES - rope))
            wq_big = (jnp.concatenate([w_abs, jnp.pad(w_pe, pe_pad), jnp.pad(_swap_halves(w_pe, rope), pe_pad)], axis=2)
                      * scale).transpose(1, 0, 2).reshape(ql, n_mla * (kl + 2 * LANES)).astype(BF16)
            w_out = ab_w_out[li]
            w_comb = _wprod(mla_w_uv[li].transpose(1, 0, 2), w_out[:n_mla * mla_v].reshape(n_mla, mla_v, d), False)
            w_comb = w_comb.reshape(n_mla * kl, d).astype(BF16)
            w_out_diff = w_out[n_mla * mla_v:].astype(BF16)
            qn, kvn = mla_q_norm[li][None], mla_kv_norm[li][None]
            lam = jnp.stack([diff_lambda_q1[li], diff_lambda_k1[li], diff_lambda_q2[li], diff_lambda_k2[li]])
            subln = diff_subln[li][None]

            cos_p, sin_p = _rope_tables(pos_p, rope // 2, 1, LANES)
            cos_s, sin_s = _rope_tables(pos_s, rope // 2, 1, LANES)
            c_p, kr_p, kcat_p, q_p, dq_p, dk_p, dv_p, dkb_p, dvb_p, kn2_p, kn2d_p = _ab_in(
                xp, cos_p, sin_p, w_aug, wq_big, qn, kvn, tm=tm_p, dims=dims)
            c_s, kr_s, _, q_s, dq_s, dk_s, dv_s, _, _, _, _ = _ab_in(
                xs, cos_s, sin_s, w_aug, wq_big, qn, kvn, tm=tm_s, dims=dims)

            o_lat_p = _mla_prompt(q_p, kcat_p, kn2_p, t_real=t_real, n_meta=n_meta, tq=tq)
            o_diff_p = _diff_prompt(dq_p, dkb_p, dvb_p, lam, subln, kn2d_p, t_real=t_real, n_meta=n_meta, tq=tq,
                                    qk=diff_qk, lambda_init=lambda_init)

            kw = kl + LANES
            q_dec = q_s.reshape(n_mla, n_seq, t_new, kw).transpose(1, 0, 2, 3).reshape(n_seq, n_mla * t_new, kw)
            dq5 = dq_s.reshape(n_diff, n_seq, t_new, 2, diff_qk).transpose(1, 0, 3, 2, 4)
            eye_h = jnp.eye(n_diff, dtype=BF16)
            eye_m = jnp.eye(2, dtype=BF16)
            q_bd = (dq5[:, :, :, :, None, None, :] * eye_h[None, :, None, None, :, None, None]
                    * eye_m[None, None, :, None, None, :, None]).reshape(n_seq, n_diff * 2 * t_new, dw)
            k_pool_t = cache_diff_k.transpose(0, 1, 3, 4, 5, 2).reshape(cache_diff_k.shape[:2] + (dw, page))
            v_pool = cache_diff_v.reshape(cache_diff_v.shape[:2] + (page * n_diff, diff_v))
            o_lat_s, o_diff_s = _paged_decode(
                q_dec, _pad_page(c_s[0], n_seq, t_new, page), _pad_page(kr_s[0], n_seq, t_new, page),
                cache_mla_latent, cache_mla_rope.transpose(0, 1, 3, 2),
                q_bd, _pad_page(dk_s[0], n_seq, t_new, page), _pad_page(dv_s[0], n_seq, t_new, page),
                lam, subln, k_pool_t, v_pool, page_table, li, t_new=t_new, n_heads=n_diff,
                pages_per_step=_tile_pages(page_table.shape[1], 32), lambda_init=lambda_init)
            o_lat_s = (o_lat_s.reshape(n_seq, n_mla, t_new, kl).transpose(0, 2, 1, 3)
                       .reshape(1, n_rows_s, n_mla * kl))
            o_diff_s = (o_diff_s.reshape(n_seq, n_diff, 2, t_new, LANES)[:, :, 0].transpose(0, 2, 1, 3)
                        .reshape(1, n_rows_s, dw).astype(BF16))

            acts_p, acts_s, ws = [o_lat_p, o_diff_p], [o_lat_s, o_diff_s], [w_comb, w_out_diff]
            for name, rows_p, rows_s in (("c", c_p, c_s), ("kr", kr_p, kr_s), ("dk", dk_p, dk_s), ("dv", dv_p, dv_s)):
                prompt_out.setdefault(name, []).append(_meta_first(rows_p, t_real, n_meta))
                sample_out.setdefault(name, []).append(rows_s.reshape(n_seq, t_new, -1))
        else:
            lj = layer // 2
            qw, vw = dims["cd_qw"], dims["cd_vw"]
            w_in = cd_w_in[lj]
            widths = [qw, qw, vw, vw, qw, qw, vw, vw, n_ml, n_ml]
            offs = [0]
            for wdt in widths:
                offs.append(offs[-1] + wdt)
            w_rq, w_rk, w_rv, w_rg, w_mq, w_mk, w_mv, w_mo, w_mi, w_mf = (w_in[:, offs[i]:offs[i + 1]] for i in range(10))
            w_aug = jnp.concatenate([w_rq, _swap_halves(w_rq, ret_dk), w_rk, _swap_halves(w_rk, ret_dk), w_rv, w_rg,
                                     w_mq, w_mk, w_mv, w_mo,
                                     _pad_cols(jnp.concatenate([w_mi, w_mf], axis=1), LANES)], axis=1).astype(BF16)
            gate_bias = _pad_cols(jnp.concatenate([ml_b_i[lj], ml_b_f[lj]])[None], LANES)
            nrm = ml_norm[lj][None]
            w_out = cd_w_out[lj].astype(BF16)
            log_gamma = jnp.log1p(-jnp.power(2.0, -5.0 - jnp.arange(n_ret, dtype=F32)))
            idx = jnp.arange(MB, dtype=F32)
            diff_ts = idx[:, None] - idx[None, :]
            dec = jnp.where(diff_ts >= 0, jnp.exp(log_gamma[:, None, None] * jnp.maximum(diff_ts, 0.0)), 0.0)
            lgr = jnp.repeat(log_gamma, ret_dk)[None]
            lgc = jnp.repeat(log_gamma, ret_dk)[:, None]

            cos_p, sin_p = _rope_tables(pos_p, ret_dk // 2, n_ret, qw)
            cos_s, sin_s = _rope_tables(pos_s, ret_dk // 2, n_ret, qw)
            rows_p = _cd_in(xp, cos_p, sin_p, w_aug, gate_bias, nrm, tm=tm_p, dims=dims)
            rows_s = _cd_in(xs, cos_s, sin_s, w_aug, gate_bias, nrm, tm=tm_s, dims=dims)

            zeros = lambda *s: jnp.zeros(s, F32)
            chunk = functools.partial(_chunk_scan, n_heads=n_ret, dk=ret_dk, dv=ret_dv)
            o_p, s_p, c_p2, n_p, m_p = chunk(
                *rows_p, dec, lgr, lgc, zeros(b, qw, ret_dv), zeros(b, ret_dv, qw), zeros(b, 1, qw), zeros(b, 1, LANES),
                blocks=[t_real // MB] + list(range(t_real // MB)), first_valid=n_meta,
                n_seq_blk=1)
            rows_s_pad = [_pad_page(r[0], n_seq, t_new, MB) for r in rows_s]
            c0 = state_mlstm_C[lj].transpose(0, 2, 1, 3).reshape(n_seq, ml_dv, qw)
            o_s, s_s, c_s2, n_s, m_s = chunk(
                *rows_s_pad, dec, lgr, lgc, state_ret[lj].reshape(n_seq, qw, ret_dv), c0,
                state_mlstm_n[lj].reshape(n_seq, 1, qw), _pad_cols(state_mlstm_m[lj], LANES)[:, None],
                blocks=[0], first_valid=t_new, n_seq_blk=1)
            o_s = o_s[:, :t_new].reshape(1, n_rows_s, 2 * vw)

            acts_p, acts_s, ws = [o_p], [o_s], [w_out]
            for dst, s_, c_, n_, m_, nb in ((prompt_out, s_p, c_p2, n_p, m_p, b), (sample_out, s_s, c_s2, n_s, m_s, n_seq)):
                dst.setdefault("s", []).append(s_.reshape(nb, n_ret, ret_dk, ret_dv))
                dst.setdefault("C", []).append(c_.reshape(nb, ml_dv, n_ml, ml_dk).transpose(0, 2, 1, 3))
                dst.setdefault("n", []).append(n_.reshape(nb, n_ml, ml_dk))
                dst.setdefault("m", []).append(m_[:, 0, :n_ml])

        if last:
            xp = _post(xp, acts_p, ws, g1, b1, g2, b2, w1, w2, tm=_tile(t_real, 512), alpha=alpha, t_out=t_real)
        else:
            xp = _post(xp, acts_p, ws, g1, b1, g2, b2, w1, w2, tm=tm_p, alpha=alpha)
        xs = _post(xs, acts_s, ws, g1, b1, g2, b2, w1, w2, tm=tm_s, alpha=alpha)

    y_prompt = xp
    y_sample = xs.reshape(n_seq, t_new, d)
    k_shape_p = (b, n_meta + t_real, n_diff, 2, diff_qk)
    v_shape_p = (b, n_meta + t_real, n_diff, diff_v)
    k_shape_s = (n_seq, t_new, n_diff, 2, diff_qk)
    v_shape_s = (n_seq, t_new, n_diff, diff_v)
    return (y_prompt, y_sample,
            jnp.stack(prompt_out["c"]), jnp.stack(prompt_out["kr"]),
            jnp.stack([a.reshape(k_shape_p) for a in prompt_out["dk"]]),
            jnp.stack([a.reshape(v_shape_p) for a in prompt_out["dv"]]),
            jnp.stack(prompt_out["s"]), jnp.stack(prompt_out["C"]), jnp.stack(prompt_out["n"]), jnp.stack(prompt_out["m"]),
            jnp.stack(sample_out["c"]), jnp.stack(sample_out["kr"]),
            jnp.stack([a.reshape(k_shape_s) for a in sample_out["dk"]]),
            jnp.stack([a.reshape(v_shape_s) for a in sample_out["dv"]]),
            jnp.stack(sample_out["s"]), jnp.stack(sample_out["C"]), jnp.stack(sample_out["n"]), jnp.stack(sample_out["m"]))


def _tile_pages(n_pages, pref):
    g = min(pref, n_pages)
    while n_pages % g:
        g -= 1
    return g
```

```python
import functools
import math

import jax
import jax.numpy as jnp
from jax import lax
from jax.experimental import pallas as pl
from jax.experimental.pallas import tpu as pltpu

F32 = jnp.float32
BF16 = jnp.bfloat16

ROPE_BASE = 10000.0
NEG_INF = -1e30
EPS = 1e-5

LANES = 128
MB = 128
HEAD_UNROLL = 4
VMEM_CAP_BYTES = 56 << 20


def _vmem(nbytes):
    return int(min(VMEM_CAP_BYTES, max(16 << 20, nbytes)))


def _params(sem, nbytes):
    return pltpu.CompilerParams(dimension_semantics=sem, vmem_limit_bytes=_vmem(nbytes))


def _const_spec(arr):
    nd = arr.ndim
    return pl.BlockSpec(arr.shape, lambda *_: (0,) * nd, pipeline_mode=pl.Buffered(1))


def _tile(n, pref):
    if n <= pref:
        return n
    best = MB
    for t in range(MB, pref + 1, MB):
        if n % t == 0:
            best = t
    assert n % best == 0, (n, pref)
    return best


def _nt(a, b):
    return lax.dot_general(a, b, (((1,), (1,)), ((), ())), preferred_element_type=F32)


def _nn(a, b):
    return jnp.dot(a, b, preferred_element_type=F32)


def _tn(a, b):
    return _nn(a.T, b)


def _rms(x):
    return x * lax.rsqrt(jnp.mean(x * x, axis=-1, keepdims=True) + EPS)


def _layer_norm(x, g, b):
    mu = jnp.mean(x, axis=-1, keepdims=True)
    xc = x - mu
    var = jnp.mean(xc * xc, axis=-1, keepdims=True)
    return xc * lax.rsqrt(var + EPS) * g + b


def _head_norm(x):
    mu = jnp.mean(x, axis=-1, keepdims=True)
    xc = x - mu
    var = jnp.mean(xc * xc, axis=-1, keepdims=True)
    return xc * lax.rsqrt(var + EPS)


LOG2E = math.log2(math.e)
EXP2_HEADROOM = 64.0
NORM_SLACK = 1.0 + 2.0 ** -6


def _lane_tile(x, width):
    reps = width // LANES
    return x if reps == 1 else jnp.concatenate([x] * reps, axis=1)


def _lane_fold(p):
    part = p[:, :LANES]
    for i in range(1, p.shape[1] // LANES):
        part = part + p[:, i * LANES:(i + 1) * LANES]
    return part


def _softmax_first(s, m_ref, l_ref, idx):
    m = jnp.broadcast_to(jnp.max(s, axis=1, keepdims=True), (s.shape[0], LANES))
    p = jnp.exp2(s - _lane_tile(m, s.shape[1]))
    m_ref[idx] = m
    l_ref[idx] = _lane_fold(p)
    return p


def _softmax_next(s, m_ref, l_ref, idx):
    m_prev = m_ref[idx]
    m_new = jnp.maximum(m_prev, jnp.max(s, axis=1, keepdims=True))
    alpha = jnp.exp2(m_prev - m_new)
    p = jnp.exp2(s - _lane_tile(m_new, s.shape[1]))
    l_ref[idx] = alpha * l_ref[idx] + _lane_fold(p)
    m_ref[idx] = m_new
    return alpha, p


def _softmax_total(l_ref, idx):
    return jnp.sum(l_ref[idx], axis=1, keepdims=True)


def _wprod_kernel(a_ref, b_ref, o_ref, *, trans_b):
    a = a_ref[...].astype(BF16)
    b = b_ref[...].astype(BF16)
    o_ref[...] = _nt(a, b) if trans_b else _nn(a, b)


def _wprod(a, b, trans_b):
    h, m, _ = a.shape
    n = b.shape[1] if trans_b else b.shape[2]
    return pl.pallas_call(
        functools.partial(_wprod_kernel, trans_b=trans_b),
        out_shape=jax.ShapeDtypeStruct((h, m, n), F32),
        grid=(h,),
        in_specs=[pl.BlockSpec((None,) + a.shape[1:], lambda i: (i, 0, 0)),
                  pl.BlockSpec((None,) + b.shape[1:], lambda i: (i, 0, 0))],
        out_specs=pl.BlockSpec((None, m, n), lambda i: (i, 0, 0)),
        compiler_params=_params(("arbitrary",), 16 << 20),
        name="weight_product",
    )(a, b)


def _ab_in_kernel(x_ref, cos_ref, sin_ref, w_ref, wq_ref, qn_ref, kvn_ref,
                  c_ref, kr_ref, kcat_ref, q_ref, dq_ref, dk_ref, dv_ref, dkb_ref, dvb_ref, kn2_ref, kn2d_ref,
                  *, ql, kl, dw, rope, n_mla, n_diff, dq_scale):
    xb = x_ref[...].astype(BF16)
    cos = cos_ref[...]
    sin = sin_ref[...]

    def mm(lo, width):
        return _nn(xb, w_ref[:, lo:lo + width])

    o_ckv, o_dq, o_dk, o_dv, o_kr = ql, ql + kl, ql + kl + dw, ql + kl + 2 * dw, ql + kl + 3 * dw
    c = _rms(mm(o_ckv, kl)) * kvn_ref[...]
    krr = mm(o_kr, 2 * LANES)
    kr = krr[:, :LANES] * cos + krr[:, LANES:] * sin
    c_ref[...] = c
    kr_ref[...] = kr[:, :rope]
    cb = c.astype(BF16)
    krb = kr.astype(BF16)
    kcat_ref[:, :kl] = cb
    kcat_ref[:, kl:] = krb
    cf = cb.astype(F32)
    krf = krb.astype(F32)
    n2 = jnp.sum(cf * cf, axis=1, keepdims=True) + jnp.sum(krf * krf, axis=1, keepdims=True)
    tile_max = jnp.broadcast_to(jnp.max(n2, axis=0, keepdims=True), kn2_ref.shape)
    dq = mm(o_dq, dw) * dq_scale
    dk = mm(o_dk, dw)
    dv = mm(o_dv, dw)
    dk_ref[...] = dk
    dv_ref[...] = dv
    n2d = None
    for h in range(n_diff):
        sl = slice(h * LANES, (h + 1) * LANES)
        dq_ref[h] = dq[:, sl].astype(BF16)
        dkh = dk[:, sl].astype(BF16)
        dkb_ref[h] = dkh
        dvb_ref[h] = dv[:, sl].astype(BF16)
        dkf = dkh.astype(F32)
        n2h = jnp.sum(dkf * dkf, axis=1, keepdims=True)
        n2d = n2h if n2d is None else jnp.maximum(n2d, n2h)
    tile_max_d = jnp.broadcast_to(jnp.max(n2d, axis=0, keepdims=True), kn2d_ref.shape)

    @pl.when(pl.program_id(1) == 0)
    def _first_tile():
        kn2_ref[...] = tile_max
        kn2d_ref[...] = tile_max_d

    @pl.when(pl.program_id(1) > 0)
    def _later_tiles():
        kn2_ref[...] = jnp.maximum(kn2_ref[...], tile_max)
        kn2d_ref[...] = jnp.maximum(kn2d_ref[...], tile_max_d)
    cq = (_rms(mm(0, ql)) * qn_ref[...]).astype(BF16)
    hw = kl + 2 * LANES
    for h in range(n_mla):
        qh = _nn(cq, wq_ref[:, h * hw:(h + 1) * hw])
        q_ref[h, :, :kl] = qh[:, :kl].astype(BF16)
        q_ref[h, :, kl:] = (qh[:, kl:kl + LANES] * cos + qh[:, kl + LANES:] * sin).astype(BF16)


def _ab_in(x, cos, sin, w_aug, wq_big, q_norm, kv_norm, *, tm, dims):
    bx, tx, d = x.shape
    ql, kl, dw, rope, n_mla, n_diff = dims["ql"], dims["kl"], dims["dw"], dims["rope"], dims["n_mla"], dims["n_diff"]
    row = lambda b, i: (b, i, 0)
    hrow = lambda b, i: (b, 0, i, 0)
    tab = lambda b, i: (i, 0)
    out_shape = (
        jax.ShapeDtypeStruct((bx, tx, kl), F32),
        jax.ShapeDtypeStruct((bx, tx, rope), F32),
        jax.ShapeDtypeStruct((bx, tx, kl + LANES), BF16),
        jax.ShapeDtypeStruct((bx, n_mla, tx, kl + LANES), BF16),
        jax.ShapeDtypeStruct((bx, n_diff, tx, LANES), BF16),
        jax.ShapeDtypeStruct((bx, tx, dw), F32),
        jax.ShapeDtypeStruct((bx, tx, dw), F32),
        jax.ShapeDtypeStruct((bx, n_diff, tx, LANES), BF16),
        jax.ShapeDtypeStruct((bx, n_diff, tx, LANES), BF16),
        jax.ShapeDtypeStruct((bx, 8, LANES), F32),
        jax.ShapeDtypeStruct((bx, 8, LANES), F32),
    )
    out_specs = (
        pl.BlockSpec((None, tm, kl), row),
        pl.BlockSpec((None, tm, rope), row),
        pl.BlockSpec((None, tm, kl + LANES), row),
        pl.BlockSpec((None, n_mla, tm, kl + LANES), hrow),
        pl.BlockSpec((None, n_diff, tm, LANES), hrow),
        pl.BlockSpec((None, tm, dw), row),
        pl.BlockSpec((None, tm, dw), row),
        pl.BlockSpec((None, n_diff, tm, LANES), hrow),
        pl.BlockSpec((None, n_diff, tm, LANES), hrow),
        pl.BlockSpec((None, 8, LANES), lambda b, i: (b, 0, 0)),
        pl.BlockSpec((None, 8, LANES), lambda b, i: (b, 0, 0)),
    )
    nbytes = (2 * tm * d * 4 + w_aug.size * 2 + wq_big.size * 2
              + 2 * tm * (kl * 4 + LANES * 4 + (kl + LANES) * 2 + n_mla * (kl + LANES) * 2
                          + 3 * n_diff * LANES * 2 + 2 * dw * 4)
              + 6 * tm * dw * 4 + (8 << 20))
    return pl.pallas_call(
        functools.partial(_ab_in_kernel, ql=ql, kl=kl, dw=dw, rope=rope, n_mla=n_mla, n_diff=n_diff,
                          dq_scale=dims["dq_scale"]),
        out_shape=out_shape,
        grid=(bx, tx // tm),
        in_specs=[pl.BlockSpec((None, tm, d), row),
                  pl.BlockSpec((tm, LANES), tab),
                  pl.BlockSpec((tm, LANES), tab),
                  _const_spec(w_aug), _const_spec(wq_big), _const_spec(q_norm), _const_spec(kv_norm)],
        out_specs=out_specs,
        compiler_params=_params(("parallel", "arbitrary"), nbytes),
        name="ab_in_proj",
    )(x, cos, sin, w_aug, wq_big, q_norm, kv_norm)


def _tri_schedule(nq):
    qi = [i for i in range(nq) for _ in range(i + 1)]
    kj = [j for i in range(nq) for j in range(i + 1)]
    return jnp.asarray(qi, jnp.int32), jnp.asarray(kj, jnp.int32)


def _mla_prompt_kernel(qi_ref, kj_ref, q_ref, k_ref, km_ref, kn2_ref, o_ref, m_sc, l_sc, acc_sc, safe_sc,
                       *, n_heads, kl, n_meta, tq):
    step = pl.program_id(1)
    qi = qi_ref[step]
    kj = kj_ref[step]

    @pl.when(kj == 0)
    def _init():
        km = km_ref[...]
        vm = km_ref[:, :kl]
        valid = lax.broadcasted_iota(jnp.int32, (tq, MB), 1) < n_meta
        kn = jnp.sqrt(jnp.max(jnp.max(kn2_ref[...], axis=1, keepdims=True), axis=0, keepdims=True)) * NORM_SLACK

        def body(i, worst):
            for u in range(HEAD_UNROLL):
                h = i * HEAD_UNROLL + u
                q = q_ref[h]
                s = jnp.where(valid, _nt(q, km), NEG_INF)
                p = _softmax_first(s, m_sc, l_sc, h)
                acc_sc[h] = _nn(p.astype(BF16), vm)
                qf = q.astype(F32)
                qn = jnp.sqrt(jnp.sum(qf * qf, axis=1, keepdims=True)) * NORM_SLACK
                worst = jnp.maximum(worst, jnp.max(qn * kn - jnp.max(s, axis=1, keepdims=True)))
            return worst

        worst = lax.fori_loop(0, n_heads // HEAD_UNROLL, body, jnp.float32(-3.0e38))
        safe_sc[0] = (worst < EXP2_HEADROOM).astype(jnp.int32)

    def block(visible):
        k = k_ref[...]
        v = k_ref[:, :kl]

        def body(i, carry):
            for u in range(HEAD_UNROLL):
                h = i * HEAD_UNROLL + u
                s = _nt(q_ref[h], k)
                if visible is not None:
                    s = jnp.where(visible, s, NEG_INF)
                alpha, p = _softmax_next(s, m_sc, l_sc, h)
                acc_sc[h] = _lane_tile(alpha, kl) * acc_sc[h] + _nn(p.astype(BF16), v)
            return carry

        lax.fori_loop(0, n_heads // HEAD_UNROLL, body, 0)

    def block_shift_by_previous_max(visible):
        k = k_ref[...]
        v = k_ref[:, :kl]

        def body(i, carry):
            for u in range(HEAD_UNROLL):
                h = i * HEAD_UNROLL + u
                s = _nt(q_ref[h], k)
                if visible is not None:
                    s = jnp.where(visible, s, NEG_INF)
                m_prev = m_sc[h]
                p = jnp.exp2(s - _lane_tile(m_prev, tq))
                m_new = jnp.maximum(m_prev, jnp.max(s, axis=1, keepdims=True))
                alpha = jnp.exp2(m_prev - m_new)
                l_sc[h] = alpha * (l_sc[h] + _lane_fold(p))
                acc_sc[h] = _lane_tile(alpha, kl) * (acc_sc[h] + _nn(p.astype(BF16), v))
                m_sc[h] = m_new
            return carry

        lax.fori_loop(0, n_heads // HEAD_UNROLL, body, 0)

    @pl.when(kj < qi)
    def _below_diagonal():
        lax.cond(safe_sc[0] == 1, lambda: block_shift_by_previous_max(None), lambda: block(None))

    @pl.when(kj == qi)
    def _diagonal():
        row = lax.broadcasted_iota(jnp.int32, (tq, tq), 0)
        col = lax.broadcasted_iota(jnp.int32, (tq, tq), 1)
        visible = col <= row
        lax.cond(safe_sc[0] == 1, lambda: block_shift_by_previous_max(visible), lambda: block(visible))
        for h in range(n_heads):
            o_ref[:, h * kl:(h + 1) * kl] = (acc_sc[h] / _softmax_total(l_sc, h)).astype(o_ref.dtype)


def _mla_meta_kernel(q_ref, km_ref, prev_ref, o_ref, *, n_heads, kl, n_meta):
    del prev_ref
    km = km_ref[...]
    vm = km_ref[:, :kl]
    row = lax.broadcasted_iota(jnp.int32, (MB, MB), 0)
    col = lax.broadcasted_iota(jnp.int32, (MB, MB), 1)
    visible = (col <= row) & (col < n_meta)
    for h in range(n_heads):
        s = jnp.where(visible, _nt(q_ref[h], km), NEG_INF)
        p = jnp.exp2(s - jnp.max(s, axis=1, keepdims=True))
        o = _nn(p.astype(BF16), vm) / jnp.sum(p, axis=1, keepdims=True)
        o_ref[:, h * kl:(h + 1) * kl] = o.astype(o_ref.dtype)


def _mla_prompt(q, kcat, kn2, *, t_real, n_meta, tq):
    b, n_heads, tp, kw = q.shape
    kl = kw - LANES
    nq = t_real // tq
    qi, kj = _tri_schedule(nq)
    meta_blk = t_real // MB
    grid_spec = pltpu.PrefetchScalarGridSpec(
        num_scalar_prefetch=2,
        grid=(b, int(qi.shape[0])),
        in_specs=[pl.BlockSpec((None, n_heads, tq, kw), lambda bb, s, qi_r, kj_r: (bb, 0, qi_r[s], 0)),
                  pl.BlockSpec((None, tq, kw), lambda bb, s, qi_r, kj_r: (bb, kj_r[s], 0)),
                  pl.BlockSpec((None, MB, kw), lambda bb, s, qi_r, kj_r: (bb, meta_blk, 0)),
                  pl.BlockSpec((None,) + kn2.shape[1:], lambda bb, s, qi_r, kj_r: (bb, 0, 0))],
        out_specs=pl.BlockSpec((None, tq, n_heads * kl), lambda bb, s, qi_r, kj_r: (bb, qi_r[s], 0)),
        scratch_shapes=[pltpu.VMEM((n_heads, tq, LANES), F32), pltpu.VMEM((n_heads, tq, LANES), F32),
                        pltpu.VMEM((n_heads, tq, kl), F32), pltpu.SMEM((1,), jnp.int32)],
    )
    nbytes = (2 * n_heads * tq * kw * 2 + 2 * tq * kw * 2 + 2 * tq * n_heads * kl * 2
              + n_heads * tq * (kl + 2 * LANES) * 4 + 8 * tq * tq * 4 + (8 << 20))
    o_real = pl.pallas_call(
        functools.partial(_mla_prompt_kernel, n_heads=n_heads, kl=kl, n_meta=n_meta, tq=tq),
        out_shape=jax.ShapeDtypeStruct((b, tp, n_heads * kl), BF16),
        grid_spec=grid_spec,
        compiler_params=_params(("parallel", "arbitrary"), nbytes),
        name="mla_prompt_attention",
    )(qi, kj, q, kcat, kcat, kn2)
    return pl.pallas_call(
        functools.partial(_mla_meta_kernel, n_heads=n_heads, kl=kl, n_meta=n_meta),
        out_shape=jax.ShapeDtypeStruct((b, tp, n_heads * kl), BF16),
        grid=(b,),
        in_specs=[pl.BlockSpec((None, n_heads, MB, kw), lambda bb: (bb, 0, meta_blk, 0)),
                  pl.BlockSpec((None, MB, kw), lambda bb: (bb, meta_blk, 0)),
                  pl.BlockSpec(memory_space=pl.ANY)],
        out_specs=pl.BlockSpec((None, MB, n_heads * kl), lambda bb: (bb, meta_blk, 0)),
        input_output_aliases={2: 0},
        compiler_params=_params(("parallel",), 16 << 20),
        name="mla_meta_attention",
    )(q, kcat, o_real)


def _lambda_diff(lam_ref, lambda_init):
    lam = lam_ref[...]
    return (jnp.exp(jnp.sum(lam[0:1] * lam[1:2], axis=1, keepdims=True))
            - jnp.exp(jnp.sum(lam[2:3] * lam[3:4], axis=1, keepdims=True)) + lambda_init)


def _diff_split(q, qk):
    lane = lax.broadcasted_iota(jnp.int32, q.shape, 1)
    zero = jnp.zeros_like(q)
    return jnp.concatenate([jnp.where(lane < qk, q, zero), jnp.where(lane >= qk, q, zero)], axis=0)


def _diff_prompt_kernel(qi_ref, kj_ref, q_ref, k_ref, v_ref, km_ref, vm_ref, lam_ref, sub_ref, kn2_ref,
                        o_ref, m_sc, l_sc, acc_sc, safe_sc, *, n_heads, qk, n_meta, tq, lambda_init):
    step = pl.program_id(1)
    qi = qi_ref[step]
    kj = kj_ref[step]

    @pl.when(kj == 0)
    def _init():
        valid = lax.broadcasted_iota(jnp.int32, (2 * tq, MB), 1) < n_meta
        kn = jnp.sqrt(jnp.max(jnp.max(kn2_ref[...], axis=1, keepdims=True), axis=0, keepdims=True)) * NORM_SLACK

        def body(i, worst):
            for u in range(HEAD_UNROLL):
                h = i * HEAD_UNROLL + u
                q2 = _diff_split(q_ref[h], qk)
                s = jnp.where(valid, _nt(q2, km_ref[h]), NEG_INF)
                p = _softmax_first(s, m_sc, l_sc, h)
                acc_sc[h] = _nn(p.astype(BF16), vm_ref[h])
                qf = q2.astype(F32)
                qn = jnp.sqrt(jnp.sum(qf * qf, axis=1, keepdims=True)) * NORM_SLACK
                worst = jnp.maximum(worst, jnp.max(qn * kn - jnp.max(s, axis=1, keepdims=True)))
            return worst

        worst = lax.fori_loop(0, n_heads // HEAD_UNROLL, body, jnp.float32(-3.0e38))
        safe_sc[0] = (worst < EXP2_HEADROOM).astype(jnp.int32)

    def block(visible):
        def body(i, carry):
            for u in range(HEAD_UNROLL):
                h = i * HEAD_UNROLL + u
                s = _nt(_diff_split(q_ref[h], qk), k_ref[h])
                if visible is not None:
                    s = jnp.where(visible, s, NEG_INF)
                alpha, p = _softmax_next(s, m_sc, l_sc, h)
                acc_sc[h] = alpha * acc_sc[h] + _nn(p.astype(BF16), v_ref[h])
            return carry

        lax.fori_loop(0, n_heads // HEAD_UNROLL, body, 0)

    def block_shift_by_previous_max(visible):
        def body(i, carry):
            for u in range(HEAD_UNROLL):
                h = i * HEAD_UNROLL + u
                s = _nt(_diff_split(q_ref[h], qk), k_ref[h])
                if visible is not None:
                    s = jnp.where(visible, s, NEG_INF)
                m_prev = m_sc[h]
                p = jnp.exp2(s - _lane_tile(m_prev, tq))
                m_new = jnp.maximum(m_prev, jnp.max(s, axis=1, keepdims=True))
                alpha = jnp.exp2(m_prev - m_new)
                l_sc[h] = alpha * (l_sc[h] + _lane_fold(p))
                acc_sc[h] = alpha * (acc_sc[h] + _nn(p.astype(BF16), v_ref[h]))
                m_sc[h] = m_new
            return carry

        lax.fori_loop(0, n_heads // HEAD_UNROLL, body, 0)

    @pl.when(kj < qi)
    def _below_diagonal():
        lax.cond(safe_sc[0] == 1, lambda: block_shift_by_previous_max(None), lambda: block(None))

    @pl.when(kj == qi)
    def _diagonal():
        row = lax.broadcasted_iota(jnp.int32, (2 * tq, tq), 0)
        row = jnp.where(row >= tq, row - tq, row)
        col = lax.broadcasted_iota(jnp.int32, (2 * tq, tq), 1)
        visible = col <= row
        lax.cond(safe_sc[0] == 1, lambda: block_shift_by_previous_max(visible), lambda: block(visible))
        lam = _lambda_diff(lam_ref, lambda_init)
        for h in range(n_heads):
            o = acc_sc[h] / _softmax_total(l_sc, h)
            o = o[:tq] - lam * o[tq:]
            o = _rms(o) * sub_ref[...] * (1.0 - lambda_init)
            o_ref[:, h * LANES:(h + 1) * LANES] = o.astype(o_ref.dtype)


def _diff_meta_kernel(q_ref, km_ref, vm_ref, lam_ref, sub_ref, prev_ref, o_ref,
                      *, n_heads, qk, n_meta, lambda_init):
    del prev_ref
    row = lax.broadcasted_iota(jnp.int32, (2 * MB, MB), 0)
    row = jnp.where(row >= MB, row - MB, row)
    col = lax.broadcasted_iota(jnp.int32, (2 * MB, MB), 1)
    visible = (col <= row) & (col < n_meta)
    lam = _lambda_diff(lam_ref, lambda_init)
    for h in range(n_heads):
        s = jnp.where(visible, _nt(_diff_split(q_ref[h], qk), km_ref[h]), NEG_INF)
        p = jnp.exp2(s - jnp.max(s, axis=1, keepdims=True))
        o = _nn(p.astype(BF16), vm_ref[h]) / jnp.sum(p, axis=1, keepdims=True)
        o = o[:MB] - lam * o[MB:]
        o = _rms(o) * sub_ref[...] * (1.0 - lambda_init)
        o_ref[:, h * LANES:(h + 1) * LANES] = o.astype(o_ref.dtype)


def _diff_prompt(q, k, v, lam, subln, kn2, *, t_real, n_meta, tq, qk, lambda_init):
    b, n_heads, tp, _ = q.shape
    nq = t_real // tq
    qi, kj = _tri_schedule(nq)
    meta_blk = t_real // MB
    qmap = lambda bb, s, qi_r, kj_r: (bb, 0, qi_r[s], 0)
    kmap = lambda bb, s, qi_r, kj_r: (bb, 0, kj_r[s], 0)
    mmap = lambda bb, s, qi_r, kj_r: (bb, 0, meta_blk, 0)
    cmap = lambda bb, s, qi_r, kj_r: (0, 0)
    grid_spec = pltpu.PrefetchScalarGridSpec(
        num_scalar_prefetch=2,
        grid=(b, int(qi.shape[0])),
        in_specs=[pl.BlockSpec((None, n_heads, tq, LANES), qmap),
                  pl.BlockSpec((None, n_heads, tq, LANES), kmap),
                  pl.BlockSpec((None, n_heads, tq, LANES), kmap),
                  pl.BlockSpec((None, n_heads, MB, LANES), mmap),
                  pl.BlockSpec((None, n_heads, MB, LANES), mmap),
                  pl.BlockSpec(lam.shape, cmap),
                  pl.BlockSpec(subln.shape, cmap),
                  pl.BlockSpec((None,) + kn2.shape[1:], lambda bb, s, qi_r, kj_r: (bb, 0, 0))],
        out_specs=pl.BlockSpec((None, tq, n_heads * LANES), lambda bb, s, qi_r, kj_r: (bb, qi_r[s], 0)),
        scratch_shapes=[pltpu.VMEM((n_heads, 2 * tq, LANES), F32), pltpu.VMEM((n_heads, 2 * tq, LANES), F32),
                        pltpu.VMEM((n_heads, 2 * tq, LANES), F32), pltpu.SMEM((1,), jnp.int32)],
    )
    nbytes = (6 * n_heads * tq * LANES * 2 + 2 * tq * n_heads * LANES * 2
              + n_heads * 2 * tq * 3 * LANES * 4 + 8 * 2 * tq * tq * 4 + (8 << 20))
    o_real = pl.pallas_call(
        functools.partial(_diff_prompt_kernel, n_heads=n_heads, qk=qk, n_meta=n_meta, tq=tq,
                          lambda_init=lambda_init),
        out_shape=jax.ShapeDtypeStruct((b, tp, n_heads * LANES), BF16),
        grid_spec=grid_spec,
        compiler_params=_params(("parallel", "arbitrary"), nbytes),
        name="diff_prompt_attention",
    )(qi, kj, q, k, v, k, v, lam, subln, kn2)
    mmap1 = lambda bb: (bb, 0, meta_blk, 0)
    return pl.pallas_call(
        functools.partial(_diff_meta_kernel, n_heads=n_heads, qk=qk, n_meta=n_meta, lambda_init=lambda_init),
        out_shape=jax.ShapeDtypeStruct((b, tp, n_heads * LANES), BF16),
        grid=(b,),
        in_specs=[pl.BlockSpec((None, n_heads, MB, LANES), mmap1),
                  pl.BlockSpec((None, n_heads, MB, LANES), mmap1),
                  pl.BlockSpec((None, n_heads, MB, LANES), mmap1),
                  pl.BlockSpec(lam.shape, lambda bb: (0, 0)),
                  pl.BlockSpec(subln.shape, lambda bb: (0, 0)),
                  pl.BlockSpec(memory_space=pl.ANY)],
        out_specs=pl.BlockSpec((None, MB, n_heads * LANES), lambda bb: (bb, meta_blk, 0)),
        input_output_aliases={5: 0},
        compiler_params=_params(("parallel",), 16 << 20),
        name="diff_meta_attention",
    )(q, k, v, lam, subln, o_real)


def _new_token_mask(rows, page, t_new):
    t = lax.broadcasted_iota(jnp.int32, (rows, page), 0) % t_new
    col = lax.broadcasted_iota(jnp.int32, (rows, page), 1)
    return col <= t


def _mla_pages_step(q_ref, cn_ref, krn_ref, lat_refs, rope_refs, o_ref, m_sc, l_sc, acc_sc, *, kl, rope, t_new):
    j = pl.program_id(1)
    q_lat = q_ref[:, :kl]
    q_pe = q_ref[:, kl:kl + rope]
    rows = q_ref.shape[0]
    page = cn_ref.shape[0]

    @pl.when(j == 0)
    def _init():
        cn = cn_ref[...].astype(BF16)
        s = _nt(q_lat, cn) + _nt(q_pe, krn_ref[...].astype(BF16))
        p = _softmax_first(jnp.where(_new_token_mask(rows, page, t_new), s, NEG_INF), m_sc, l_sc, 0)
        acc_sc[...] = _nn(p.astype(BF16), cn)

    lat = jnp.concatenate([r[...].astype(BF16) for r in lat_refs], axis=0)
    kr_t = jnp.concatenate([r[...].astype(BF16) for r in rope_refs], axis=1)
    alpha, p = _softmax_next(_nt(q_lat, lat) + _nn(q_pe, kr_t), m_sc, l_sc, 0)
    acc_sc[...] = _lane_tile(alpha, kl) * acc_sc[...] + _nn(p.astype(BF16), lat)

    @pl.when(j == pl.num_programs(1) - 1)
    def _finish():
        o_ref[...] = (acc_sc[...] / _softmax_total(l_sc, 0)).astype(o_ref.dtype)


def _page_specs(pool, li, n_pages, n_tab):
    blk = (None, None) + pool.shape[2:]
    specs = []
    for g in range(n_pages):
        specs.append(pl.BlockSpec(
            blk, lambda b, j, pt, g=g: (li, pt[b * n_tab + j * n_pages + g]) + (0,) * (pool.ndim - 2)))
    return specs


def _diff_pages_step(q_ref, kn_ref, vn_ref, lam_ref, sub_ref, k_refs, v_refs, o_ref, m_sc, l_sc, acc_sc,
                     *, n_heads, t_new, lambda_init):
    j = pl.program_id(1)
    q = q_ref[...]
    rows, width = q.shape
    page = kn_ref.shape[0]

    @pl.when(j == 0)
    def _init():
        s = _nt(q, kn_ref[...].astype(BF16))
        p = _softmax_first(jnp.where(_new_token_mask(rows, page, t_new), s, NEG_INF), m_sc, l_sc, 0)
        acc_sc[...] = _nn(p.astype(BF16), vn_ref[...].astype(BF16))

    k_t = jnp.concatenate([r[...].astype(BF16) for r in k_refs], axis=1)
    alpha, p = _softmax_next(_nn(q, k_t), m_sc, l_sc, 0)
    v = jnp.concatenate(
        [jnp.concatenate([r[pl.ds(h, page, stride=n_heads), :].astype(BF16) for h in range(n_heads)], axis=1)
         for r in v_refs], axis=0)
    acc_sc[...] = _lane_tile(alpha, width) * acc_sc[...] + _nn(p.astype(BF16), v)

    @pl.when(j == pl.num_programs(1) - 1)
    def _finish():
        lam = _lambda_diff(lam_ref, lambda_init)
        o = acc_sc[...] / _softmax_total(l_sc, 0)
        per_head = rows // n_heads
        for h in range(n_heads):
            blk = o[h * per_head:(h + 1) * per_head, h * LANES:(h + 1) * LANES]
            d = blk - lam * pltpu.roll(blk, per_head // 2, 0)
            d = _rms(d) * sub_ref[...] * (1.0 - lambda_init)
            o_ref[h * per_head:(h + 1) * per_head, :] = d.astype(o_ref.dtype)


def _paged_decode_kernel(pt_ref, qm_ref, cn_ref, krn_ref, qd_ref, kn_ref, vn_ref, lam_ref, sub_ref, *rest,
                         n_pages, kl, rope, n_heads, t_new, lambda_init):
    del pt_ref
    g = n_pages
    lat_refs, rope_refs, k_refs, v_refs = rest[:g], rest[g:2 * g], rest[2 * g:3 * g], rest[3 * g:4 * g]
    om_ref, od_ref, mm_sc, lm_sc, am_sc, md_sc, ld_sc, ad_sc = rest[4 * g:]
    _mla_pages_step(qm_ref, cn_ref, krn_ref, lat_refs, rope_refs, om_ref, mm_sc, lm_sc, am_sc,
                    kl=kl, rope=rope, t_new=t_new)
    _diff_pages_step(qd_ref, kn_ref, vn_ref, lam_ref, sub_ref, k_refs, v_refs, od_ref, md_sc, ld_sc, ad_sc,
                     n_heads=n_heads, t_new=t_new, lambda_init=lambda_init)


def _paged_decode(q_mla, c_new, kr_new, lat_pool, rope_pool_t, q_diff, k_new, v_new, lam, subln, k_pool_t, v_pool,
                  page_table, li, *, t_new, n_heads, pages_per_step, lambda_init):
    n_seq, rows, kw = q_mla.shape
    kl = kw - LANES
    rope, page = rope_pool_t.shape[2], rope_pool_t.shape[3]
    rows_d, width = q_diff.shape[1], q_diff.shape[2]
    n_tab = page_table.shape[1]
    g = pages_per_step
    seq3 = lambda b, j, pt: (b, 0, 0)
    cst = lambda b, j, pt: (0, 0)
    grid_spec = pltpu.PrefetchScalarGridSpec(
        num_scalar_prefetch=1,
        grid=(n_seq, n_tab // g),
        in_specs=[pl.BlockSpec((None, rows, kw), seq3),
                  pl.BlockSpec((None, page, kl), seq3),
                  pl.BlockSpec((None, page, rope), seq3),
                  pl.BlockSpec((None, rows_d, width), seq3),
                  pl.BlockSpec((None, page, width), seq3),
                  pl.BlockSpec((None, page, width), seq3),
                  pl.BlockSpec(lam.shape, cst),
                  pl.BlockSpec(subln.shape, cst)]
                 + _page_specs(lat_pool, li, g, n_tab) + _page_specs(rope_pool_t, li, g, n_tab)
                 + _page_specs(k_pool_t, li, g, n_tab) + _page_specs(v_pool, li, g, n_tab),
        out_specs=[pl.BlockSpec((None, rows, kl), seq3), pl.BlockSpec((None, rows_d, LANES), seq3)],
        scratch_shapes=[pltpu.VMEM((1, rows, LANES), F32), pltpu.VMEM((1, rows, LANES), F32),
                        pltpu.VMEM((rows, kl), F32),
                        pltpu.VMEM((1, rows_d, LANES), F32), pltpu.VMEM((1, rows_d, LANES), F32),
                        pltpu.VMEM((rows_d, width), F32)],
    )
    page_bytes = page * (kl + rope + 2 * width)
    nbytes = 2 * g * page_bytes * 4 + 2 * g * page_bytes * 2 + (8 << 20)
    return pl.pallas_call(
        functools.partial(_paged_decode_kernel, n_pages=g, kl=kl, rope=rope, n_heads=n_heads, t_new=t_new,
                          lambda_init=lambda_init),
        out_shape=[jax.ShapeDtypeStruct((n_seq, rows, kl), BF16), jax.ShapeDtypeStruct((n_seq, rows_d, LANES), F32)],
        grid_spec=grid_spec,
        compiler_params=_params(("parallel", "arbitrary"), nbytes),
        name="paged_attention",
    )(page_table.reshape(-1), q_mla, c_new, kr_new, q_diff, k_new, v_new, lam, subln,
      *([lat_pool] * g), *([rope_pool_t] * g), *([k_pool_t] * g), *([v_pool] * g))


def _post_kernel(*refs, n_act, alpha, tf):
    x_ref = refs[0]
    act_refs = refs[1:1 + n_act]
    w_refs = refs[1 + n_act:1 + 2 * n_act]
    g1_ref, b1_ref, g2_ref, b2_ref, w1_ref, w2_ref, o_ref = refs[1 + 2 * n_act:]
    mix = _nn(act_refs[0][...], w_refs[0][...])
    for a_ref, w_ref in zip(act_refs[1:], w_refs[1:]):
        mix = mix + _nn(a_ref[...], w_ref[...])
    x1 = _layer_norm(alpha * x_ref[...] + mix, g1_ref[...], b1_ref[...])
    x1b = x1.astype(BF16)
    d_ff = w1_ref.shape[1]
    acc = None
    for c in range(d_ff // tf):
        hid = jnp.maximum(_nn(x1b, w1_ref[:, c * tf:(c + 1) * tf]), 0.0)
        part = _nn((hid * hid).astype(BF16), w2_ref[c * tf:(c + 1) * tf, :])
        acc = part if acc is None else acc + part
    o_ref[...] = _layer_norm(alpha * x1 + acc, g2_ref[...], b2_ref[...])


def _post(x, acts, ws, g1, b1, g2, b2, w1, w2, *, tm, alpha, t_out=None):
    bx, tx, d = x.shape
    t_out = tx if t_out is None else t_out
    row = lambda b, i: (b, i, 0)
    tf = min(512, w1.shape[1])
    nbytes = (4 * tm * d * 4 + sum(2 * tm * a.shape[2] * 2 for a in acts) + sum(w.size * 2 for w in ws)
              + (w1.size + w2.size) * 2 + 6 * tm * d * 4 + 4 * tm * tf * 4 + (8 << 20))
    return pl.pallas_call(
        functools.partial(_post_kernel, n_act=len(acts), alpha=alpha, tf=tf),
        out_shape=jax.ShapeDtypeStruct((bx, t_out, d), F32),
        grid=(bx, t_out // tm),
        in_specs=[pl.BlockSpec((None, tm, d), row)]
                 + [pl.BlockSpec((None, tm, a.shape[2]), row) for a in acts]
                 + [_const_spec(w) for w in ws]
                 + [_const_spec(v) for v in (g1, b1, g2, b2, w1, w2)],
        out_specs=pl.BlockSpec((None, tm, d), row),
        compiler_params=_params(("parallel", "parallel"), nbytes),
        name="out_proj_mlp",
    )(x, *acts, *ws, g1, b1, g2, b2, w1, w2)


def _log_sigmoid(x):
    return -(jnp.maximum(-x, 0.0) + jnp.log1p(jnp.exp(-jnp.abs(x))))


def _cd_in_kernel(x_ref, cos_ref, sin_ref, w_ref, gb_ref, nrm_ref,
                  rq_ref, rk_ref, rv_ref, rg_ref, mq_ref, mk_ref, mv_ref, mg_ref, gt_ref,
                  *, qw, vw, n_ml, rk_scale, mk_scale):
    xb = x_ref[...].astype(BF16)
    cos = cos_ref[...]
    sin = sin_ref[...]

    def mm(lo, width):
        return _nn(xb, w_ref[:, lo:lo + width])

    rq_ref[...] = mm(0, qw) * cos + mm(qw, qw) * sin
    rk_ref[...] = (mm(2 * qw, qw) * cos + mm(3 * qw, qw) * sin) * rk_scale
    o = 4 * qw
    rv_ref[...] = mm(o, vw).astype(BF16)
    rg = mm(o + vw, vw)
    rg_ref[...] = rg * jax.nn.sigmoid(rg)
    o = o + 2 * vw
    mq_ref[...] = mm(o, qw)
    mk_ref[...] = mm(o + qw, qw) * mk_scale
    o = o + 2 * qw
    mv_ref[...] = mm(o, vw).astype(BF16)
    mg_ref[...] = jax.nn.sigmoid(mm(o + vw, vw)) * nrm_ref[...]
    pre = mm(o + 2 * vw, LANES) + gb_ref[...]
    lane = lax.broadcasted_iota(jnp.int32, pre.shape, 1)
    gt_ref[...] = jnp.where(lane < n_ml, pre, _log_sigmoid(pre))


def _cd_in(x, cos, sin, w_aug, gate_bias, ml_norm, *, tm, dims):
    bx, tx, d = x.shape
    qw, vw = dims["cd_qw"], dims["cd_vw"]
    row = lambda b, i: (b, i, 0)
    tab = lambda b, i: (i, 0)
    shapes = [(qw, F32), (qw, F32), (vw, BF16), (vw, F32), (qw, F32), (qw, F32), (vw, BF16), (vw, F32), (LANES, F32)]
    nbytes = (2 * tm * d * 4 + w_aug.size * 2 + 2 * tm * sum(w * (4 if t == F32 else 2) for w, t in shapes)
              + 8 * tm * vw * 4 + (8 << 20))
    return pl.pallas_call(
        functools.partial(_cd_in_kernel, qw=qw, vw=vw, n_ml=dims["n_ml"],
                          rk_scale=dims["rk_scale"], mk_scale=dims["mk_scale"]),
        out_shape=tuple(jax.ShapeDtypeStruct((bx, tx, w), t) for w, t in shapes),
        grid=(bx, tx // tm),
        in_specs=[pl.BlockSpec((None, tm, d), row),
                  pl.BlockSpec((tm, qw), tab), pl.BlockSpec((tm, qw), tab),
                  _const_spec(w_aug), _const_spec(gate_bias), _const_spec(ml_norm)],
        out_specs=tuple(pl.BlockSpec((None, tm, w), row) for w, _ in shapes),
        compiler_params=_params(("parallel", "parallel"), nbytes),
        name="cd_in_proj",
    )(x, cos, sin, w_aug, gate_bias, ml_norm)


def _split_bf16(x):
    hi = x.astype(BF16)
    lo = (x - hi.astype(F32)).astype(BF16)
    return hi, lo


def _chunk_kernel(ids_ref, rq_ref, rk_ref, rv_ref, rg_ref, mq_ref, mk_ref, mv_ref, mg_ref, gt_ref,
                  dec_ref, lgr_ref, lgc_ref, s0_ref, c0_ref, n0_ref, m0_ref,
                  o_ref, s_out, c_out, n_out, m_out,
                  s_sc, c_sc, n_sc, m_sc,
                  *, n_heads, dk, dv, first_valid, n_seq_blk):
    del ids_ref
    ci = pl.program_id(1)

    @pl.when(ci == 0)
    def _load_state():
        s_sc[...] = s0_ref[...]
        c_sc[...] = c0_ref[...]
        n_sc[...] = n0_ref[...]
        m_sc[...] = m0_ref[...]

    n_valid = jnp.where(ci == 0, first_valid, MB)
    for bi in range(n_seq_blk):
        _chunk_step(bi, n_valid, rq_ref, rk_ref, rv_ref, rg_ref, mq_ref, mk_ref, mv_ref, mg_ref, gt_ref,
                    dec_ref, lgr_ref, lgc_ref, o_ref, s_sc, c_sc, n_sc, m_sc, n_heads=n_heads, dk=dk, dv=dv)

    @pl.when(ci == pl.num_programs(1) - 1)
    def _store_state():
        s_out[...] = s_sc[...]
        c_out[...] = c_sc[...]
        n_out[...] = n_sc[...]
        m_out[...] = m_sc[...]


def _chunk_step(bi, n_valid, rq_ref, rk_ref, rv_ref, rg_ref, mq_ref, mk_ref, mv_ref, mg_ref, gt_ref,
                dec_ref, lgr_ref, lgc_ref, o_ref, s_sc, c_sc, n_sc, m_sc, *, n_heads, dk, dv):
    qw = n_heads * dk
    idx_c = lax.broadcasted_iota(jnp.int32, (MB, 1), 0)
    live_c = idx_c < n_valid
    lane_q = lax.broadcasted_iota(jnp.int32, (MB, qw), 1)
    t_f = idx_c.astype(F32)
    nv_f = n_valid.astype(F32)

    lgr = lgr_ref[...]
    rq = rq_ref[bi]
    rk = jnp.where(live_c, rk_ref[bi], 0.0)
    rv = rv_ref[bi]
    q_dec = rq * jnp.exp(lgr * (t_f + 1.0))
    k_dec = (rk * jnp.exp(lgr * jnp.where(live_c, nv_f - 1.0 - t_f, 0.0))).astype(BF16)
    rk_b = rk.astype(BF16)
    s_all = s_sc[bi]
    s_b = s_all.astype(BF16)
    upd = _tn(k_dec, rv)
    s_dec = jnp.exp(lgc_ref[...] * nv_f)
    for h in range(n_heads):
        head = (lane_q >= h * dk) & (lane_q < (h + 1) * dk)
        inner = _nt(jnp.where(head, rq, 0.0).astype(BF16), rk_b) * dec_ref[h]
        o = _nn(inner.astype(BF16), rv[:, h * dv:(h + 1) * dv]) + _nn(jnp.where(head, q_dec, 0.0).astype(BF16), s_b)
        o = _head_norm(o) * rg_ref[bi, :, h * dv:(h + 1) * dv]
        o_ref[bi, :, h * dv:(h + 1) * dv] = o.astype(o_ref.dtype)
        s_sc[bi, h * dk:(h + 1) * dk, :] = (s_dec[h * dk:(h + 1) * dk] * s_all[h * dk:(h + 1) * dk]
                                            + upd[h * dk:(h + 1) * dk, h * dv:(h + 1) * dv])

    gt = gt_ref[bi]
    lane_g = lax.broadcasted_iota(jnp.int32, gt.shape, 1)
    gt = jnp.where(live_c, gt, jnp.where(lane_g < n_heads, NEG_INF, 0.0))
    row_i = lax.broadcasted_iota(jnp.int32, (MB, MB), 0)
    col_i = lax.broadcasted_iota(jnp.int32, (MB, MB), 1)
    causal = col_i <= row_i
    tri = causal.astype(BF16)
    g_hi, g_lo = _split_bf16(gt)
    cum_c = _nn(tri, g_hi) + _nn(tri, g_lo)
    gt_r = gt.T
    cum_r = cum_c.T
    mq = mq_ref[bi]
    mk = jnp.where(live_c, mk_ref[bi], 0.0)
    mk_b = mk.astype(BF16)
    mv = mv_ref[bi]
    c_cat = c_sc[bi]
    c_b = c_cat.astype(BF16)
    n_cat = n_sc[bi]
    m_all = m_sc[bi]
    c_new = jnp.zeros_like(c_cat)
    n_new = jnp.zeros_like(n_cat)
    g_keep = jnp.zeros_like(n_cat)
    m_new = jnp.zeros_like(m_all)
    lane_m = lax.broadcasted_iota(jnp.int32, m_all.shape, 1)
    lane_n = lax.broadcasted_iota(jnp.int32, n_cat.shape, 1)
    for h in range(n_heads):
        head = (lane_q >= h * dk) & (lane_q < (h + 1) * dk)
        head_n = (lane_n >= h * dk) & (lane_n < (h + 1) * dk)
        b_c = cum_c[:, n_heads + h:n_heads + h + 1]
        i_c = gt[:, h:h + 1]
        b_r = cum_r[n_heads + h:n_heads + h + 1, :]
        i_r = gt_r[h:h + 1, :]
        m_prev = m_all[:, h:h + 1]
        d = jnp.where(causal, b_c - b_r + i_r, NEG_INF)
        g = b_c + m_prev
        m_t = jnp.maximum(g, jnp.max(d, axis=1, keepdims=True))
        w = jnp.exp(d - m_t)
        gs = jnp.exp(g - m_t)
        qh = jnp.where(head, mq, 0.0)
        qh_b = qh.astype(BF16)
        a = _nt(qh_b, mk_b) * w
        num = _nn(a.astype(BF16), mv[:, h * dv:(h + 1) * dv]) + gs * _nt(qh_b, c_b)
        den = jnp.sum(a, axis=1, keepdims=True) + gs * jnp.sum(qh * n_cat, axis=1, keepdims=True)
        hid = num / jnp.maximum(jnp.abs(den), jnp.exp(-m_t))
        hid = _head_norm(hid) * mg_ref[bi, :, h * dv:(h + 1) * dv]
        o_ref[bi, :, n_heads * dv + h * dv:n_heads * dv + (h + 1) * dv] = hid.astype(o_ref.dtype)
        m_last = m_t[MB - 1:MB]
        w_last = jnp.exp(b_c[MB - 1:MB] - b_c + i_c - m_last)
        g_last = gs[MB - 1:MB]
        kh = jnp.where(head, mk, 0.0)
        c_new = c_new + _tn((mv[:, h * dv:(h + 1) * dv].astype(F32) * w_last).astype(BF16), kh.astype(BF16))
        n_new = n_new + jnp.sum(kh * w_last, axis=0, keepdims=True)
        g_keep = jnp.where(head_n, g_last, g_keep)
        m_new = jnp.where(lane_m == h, m_last, m_new)
    c_sc[bi] = g_keep * c_cat + c_new
    n_sc[bi] = g_keep * n_cat + n_new
    m_sc[bi] = m_new


def _chunk_scan(rq, rk, rv, rg, mq, mk, mv, mg, gt, dec, lgr, lgc, s0, c0, n0, m0,
                *, n_heads, dk, dv, blocks, first_valid, n_seq_blk):
    bx, tx, qw = rq.shape
    vw = rv.shape[2]
    nb = n_seq_blk
    assert bx % nb == 0
    blk_ids = jnp.asarray(blocks, jnp.int32)
    row = lambda b, c, ids: (b, ids[c], 0)
    st = lambda b, c, ids: (b, 0, 0)
    cst3 = lambda b, c, ids: (0, 0, 0)
    cst2 = lambda b, c, ids: (0, 0)
    state_shapes = [(qw, dv), (dv, qw), (1, qw), (1, LANES)]
    grid_spec = pltpu.PrefetchScalarGridSpec(
        num_scalar_prefetch=1,
        grid=(bx // nb, len(blocks)),
        in_specs=[pl.BlockSpec((nb, MB, qw), row), pl.BlockSpec((nb, MB, qw), row),
                  pl.BlockSpec((nb, MB, vw), row), pl.BlockSpec((nb, MB, vw), row),
                  pl.BlockSpec((nb, MB, qw), row), pl.BlockSpec((nb, MB, qw), row),
                  pl.BlockSpec((nb, MB, vw), row), pl.BlockSpec((nb, MB, vw), row),
                  pl.BlockSpec((nb, MB, LANES), row),
                  pl.BlockSpec(dec.shape, cst3), pl.BlockSpec(lgr.shape, cst2), pl.BlockSpec(lgc.shape, cst2)]
                 + [pl.BlockSpec((nb,) + s, st) for s in state_shapes],
        out_specs=[pl.BlockSpec((nb, MB, 2 * vw), row)] + [pl.BlockSpec((nb,) + s, st) for s in state_shapes],
        scratch_shapes=[pltpu.VMEM((nb,) + s, F32) for s in state_shapes],
    )
    out_shape = [jax.ShapeDtypeStruct((bx, tx, 2 * vw), BF16)] + [jax.ShapeDtypeStruct((bx,) + s, F32) for s in state_shapes]
    return pl.pallas_call(
        functools.partial(_chunk_kernel, n_heads=n_heads, dk=dk, dv=dv, first_valid=first_valid, n_seq_blk=nb),
        out_shape=out_shape,
        grid_spec=grid_spec,
        compiler_params=_params(("parallel", "arbitrary"), 32 << 20),
        name="retention_mlstm_chunks",
    )(blk_ids, rq, rk, rv, rg, mq, mk, mv, mg, gt, dec, lgr, lgc, s0, c0, n0, m0)


def _rope_tables(pos, half, reps, width):
    freqs = ROPE_BASE ** (-jnp.arange(half, dtype=F32) / half)
    ang = pos.astype(F32)[:, None] * freqs[None, :]
    cos, sin = jnp.cos(ang), jnp.sin(ang)
    cos = jnp.tile(jnp.concatenate([cos, cos], axis=1), (1, reps))
    sin = jnp.tile(jnp.concatenate([-sin, sin], axis=1), (1, reps))
    pad = width - cos.shape[1]
    return jnp.pad(cos, ((0, 0), (0, pad))), jnp.pad(sin, ((0, 0), (0, pad)))


def _swap_halves(w, group):
    shp = w.shape
    w = w.reshape(shp[:-1] + (shp[-1] // group, 2, group // 2))
    return w[..., ::-1, :].reshape(shp)


def _pad_cols(w, width):
    return jnp.pad(w, ((0, 0), (0, width - w.shape[1])))


def _pad_page(rows, n_seq, t_new, page):
    w = rows.shape[-1]
    return jnp.pad(rows.reshape(n_seq, t_new, w), ((0, 0), (0, page - t_new), (0, 0)))


def _meta_first(a, t_real, n_meta):
    return jnp.concatenate([a[:, t_real:t_real + n_meta], a[:, :t_real]], axis=1)


def kernel(x_prompt, x_sample, cache_mla_latent, cache_mla_rope, cache_diff_k, cache_diff_v, state_ret, state_mlstm_C, state_mlstm_n, state_mlstm_m, page_table, meta_tokens, ab_w_in, mla_q_norm, mla_w_uq, mla_kv_norm, mla_w_uk, mla_w_uv, diff_lambda_q1, diff_lambda_k1, diff_lambda_q2, diff_lambda_k2, diff_subln, ab_w_out, cd_w_in, ml_b_i, ml_b_f, ml_norm, cd_w_out, ln1_g, ln1_b, ln2_g, ln2_b, mlp_w1, mlp_w2):
    b, t_real, d = x_prompt.shape
    n_seq, t_new, _ = x_sample.shape
    n_meta = meta_tokens.shape[0]
    depth = ln1_g.shape[0]
    alpha = (2 * depth) ** 0.25
    page = cache_mla_latent.shape[2]
    past_len = page_table.shape[1] * page
    ql, kl, rope = mla_q_norm.shape[1], mla_kv_norm.shape[1], cache_mla_rope.shape[3]
    n_mla, nope, mla_v = mla_w_uk.shape[2], mla_w_uk.shape[3], mla_w_uv.shape[3]
    n_diff, diff_qk, diff_v = cache_diff_k.shape[3], cache_diff_k.shape[5], cache_diff_v.shape[4]
    n_ret, ret_dk, ret_dv = state_ret.shape[2], state_ret.shape[3], state_ret.shape[4]
    n_ml, ml_dv, ml_dk = state_mlstm_C.shape[2], state_mlstm_C.shape[3], state_mlstm_C.shape[4]
    assert 2 * diff_qk == LANES and diff_v == LANES and n_meta <= MB and t_real % MB == 0 and page == MB
    assert (n_ret, ret_dk, ret_dv) == (n_ml, ml_dk, ml_dv) and ret_dv == LANES and t_new <= MB
    dw = n_diff * LANES
    dims = dict(ql=ql, kl=kl, dw=dw, rope=rope, n_mla=n_mla, n_diff=n_diff, dq_scale=diff_qk ** -0.5 * LOG2E,
                cd_qw=n_ret * ret_dk, cd_vw=n_ret * ret_dv, n_ml=n_ml, rk_scale=ret_dk ** -0.5, mk_scale=ml_dk ** -0.5)
    tp = t_real + MB
    n_rows_s = n_seq * t_new
    tm_p = _tile(tp, 640)
    tm_s = _tile(n_rows_s, 512)
    tq = _tile(t_real, 512)

    xp = jnp.concatenate([x_prompt, jnp.broadcast_to(meta_tokens.astype(x_prompt.dtype)[None], (b, n_meta, d)),
                          jnp.zeros((b, MB - n_meta, d), x_prompt.dtype)], axis=1)
    xs = x_sample.reshape(1, n_rows_s, d)
    pos_p = jnp.concatenate([n_meta + jnp.arange(t_real), jnp.arange(MB)])
    pos_s = past_len + (jnp.arange(n_rows_s) % t_new)

    prompt_out, sample_out = {}, {}
    for layer in range(depth):
        g1, b1, g2, b2 = (v[layer][None] for v in (ln1_g, ln1_b, ln2_g, ln2_b))
        w1, w2 = mlp_w1[layer].astype(BF16), mlp_w2[layer].astype(BF16)
        last = layer == depth - 1
        if layer % 2 == 0:
            li = layer // 2
            lambda_init = 0.8 - 0.6 * math.exp(-0.3 * layer)
            scale = (nope + rope) ** -0.5 * LOG2E
            w_in = ab_w_in[li]
            o = [0, ql, ql + kl, ql + kl + rope, ql + kl + rope + dw, ql + kl + rope + 2 * dw, ql + kl + rope + 3 * dw]
            w_cq, w_ckv, w_kr, w_dq, w_dk, w_dv = (w_in[:, o[i]:o[i + 1]] for i in range(6))
            w_aug = jnp.concatenate([w_cq, w_ckv, w_dq, w_dk, w_dv, _pad_cols(w_kr, LANES),
                                     _pad_cols(_swap_halves(w_kr, rope), LANES)], axis=1).astype(BF16)
            w_uq = mla_w_uq[li].reshape(ql, n_mla, nope + rope)
            w_abs = _wprod(w_uq[:, :, :nope].transpose(1, 0, 2), mla_w_uk[li].transpose(1, 0, 2), True)
            w_pe = w_uq[:, :, nope:].transpose(1, 0, 2)
            pe_pad = ((0, 0), (0, 0), (0, LANES - rope))
            wq_big = (jnp.concatenate([w_abs, jnp.pad(w_pe, pe_pad), jnp.pad(_swap_halves(w_pe, rope), pe_pad)], axis=2)
                      * scale).transpose(1, 0, 2).reshape(ql, n_mla * (kl + 2 * LANES)).astype(BF16)
            w_out = ab_w_out[li]
            w_comb = _wprod(mla_w_uv[li].transpose(1, 0, 2), w_out[:n_mla * mla_v].reshape(n_mla, mla_v, d), False)
            w_comb = w_comb.reshape(n_mla * kl, d).astype(BF16)
            w_out_diff = w_out[n_mla * mla_v:].astype(BF16)
            qn, kvn = mla_q_norm[li][None], mla_kv_norm[li][None]
            lam = jnp.stack([diff_lambda_q1[li], diff_lambda_k1[li], diff_lambda_q2[li], diff_lambda_k2[li]])
            subln = diff_subln[li][None]

            cos_p, sin_p = _rope_tables(pos_p, rope // 2, 1, LANES)
            cos_s, sin_s = _rope_tables(pos_s, rope // 2, 1, LANES)
            c_p, kr_p, kcat_p, q_p, dq_p, dk_p, dv_p, dkb_p, dvb_p, kn2_p, kn2d_p = _ab_in(
                xp, cos_p, sin_p, w_aug, wq_big, qn, kvn, tm=tm_p, dims=dims)
            c_s, kr_s, _, q_s, dq_s, dk_s, dv_s, _, _, _, _ = _ab_in(
                xs, cos_s, sin_s, w_aug, wq_big, qn, kvn, tm=tm_s, dims=dims)

            o_lat_p = _mla_prompt(q_p, kcat_p, kn2_p, t_real=t_real, n_meta=n_meta, tq=tq)
            o_diff_p = _diff_prompt(dq_p, dkb_p, dvb_p, lam, subln, kn2d_p, t_real=t_real, n_meta=n_meta, tq=tq,
                                    qk=diff_qk, lambda_init=lambda_init)

            kw = kl + LANES
            q_dec = q_s.reshape(n_mla, n_seq, t_new, kw).transpose(1, 0, 2, 3).reshape(n_seq, n_mla * t_new, kw)
            dq5 = dq_s.reshape(n_diff, n_seq, t_new, 2, diff_qk).transpose(1, 0, 3, 2, 4)
            eye_h = jnp.eye(n_diff, dtype=BF16)
            eye_m = jnp.eye(2, dtype=BF16)
            q_bd = (dq5[:, :, :, :, None, None, :] * eye_h[None, :, None, None, :, None, None]
                    * eye_m[None, None, :, None, None, :, None]).reshape(n_seq, n_diff * 2 * t_new, dw)
            k_pool_t = cache_diff_k.transpose(0, 1, 3, 4, 5, 2).reshape(cache_diff_k.shape[:2] + (dw, page))
            v_pool = cache_diff_v.reshape(cache_diff_v.shape[:2] + (page * n_diff, diff_v))
            o_lat_s, o_diff_s = _paged_decode(
                q_dec, _pad_page(c_s[0], n_seq, t_new, page), _pad_page(kr_s[0], n_seq, t_new, page),
                cache_mla_latent, cache_mla_rope.transpose(0, 1, 3, 2),
                q_bd, _pad_page(dk_s[0], n_seq, t_new, page), _pad_page(dv_s[0], n_seq, t_new, page),
                lam, subln, k_pool_t, v_pool, page_table, li, t_new=t_new, n_heads=n_diff,
                pages_per_step=_tile_pages(page_table.shape[1], 32), lambda_init=lambda_init)
            o_lat_s = (o_lat_s.reshape(n_seq, n_mla, t_new, kl).transpose(0, 2, 1, 3)
                       .reshape(1, n_rows_s, n_mla * kl))
            o_diff_s = (o_diff_s.reshape(n_seq, n_diff, 2, t_new, LANES)[:, :, 0].transpose(0, 2, 1, 3)
                        .reshape(1, n_rows_s, dw).astype(BF16))

            acts_p, acts_s, ws = [o_lat_p, o_diff_p], [o_lat_s, o_diff_s], [w_comb, w_out_diff]
            for name, rows_p, rows_s in (("c", c_p, c_s), ("kr", kr_p, kr_s), ("dk", dk_p, dk_s), ("dv", dv_p, dv_s)):
                prompt_out.setdefault(name, []).append(_meta_first(rows_p, t_real, n_meta))
                sample_out.setdefault(name, []).append(rows_s.reshape(n_seq, t_new, -1))
        else:
            lj = layer // 2
            qw, vw = dims["cd_qw"], dims["cd_vw"]
            w_in = cd_w_in[lj]
            widths = [qw, qw, vw, vw, qw, qw, vw, vw, n_ml, n_ml]
            offs = [0]
            for wdt in widths:
                offs.append(offs[-1] + wdt)
            w_rq, w_rk, w_rv, w_rg, w_mq, w_mk, w_mv, w_mo, w_mi, w_mf = (w_in[:, offs[i]:offs[i + 1]] for i in range(10))
            w_aug = jnp.concatenate([w_rq, _swap_halves(w_rq, ret_dk), w_rk, _swap_halves(w_rk, ret_dk), w_rv, w_rg,
                                     w_mq, w_mk, w_mv, w_mo,
                                     _pad_cols(jnp.concatenate([w_mi, w_mf], axis=1), LANES)], axis=1).astype(BF16)
            gate_bias = _pad_cols(jnp.concatenate([ml_b_i[lj], ml_b_f[lj]])[None], LANES)
            nrm = ml_norm[lj][None]
            w_out = cd_w_out[lj].astype(BF16)
            log_gamma = jnp.log1p(-jnp.power(2.0, -5.0 - jnp.arange(n_ret, dtype=F32)))
            idx = jnp.arange(MB, dtype=F32)
            diff_ts = idx[:, None] - idx[None, :]
            dec = jnp.where(diff_ts >= 0, jnp.exp(log_gamma[:, None, None] * jnp.maximum(diff_ts, 0.0)), 0.0)
            lgr = jnp.repeat(log_gamma, ret_dk)[None]
            lgc = jnp.repeat(log_gamma, ret_dk)[:, None]

            cos_p, sin_p = _rope_tables(pos_p, ret_dk // 2, n_ret, qw)
            cos_s, sin_s = _rope_tables(pos_s, ret_dk // 2, n_ret, qw)
            rows_p = _cd_in(xp, cos_p, sin_p, w_aug, gate_bias, nrm, tm=tm_p, dims=dims)
            rows_s = _cd_in(xs, cos_s, sin_s, w_aug, gate_bias, nrm, tm=tm_s, dims=dims)

            zeros = lambda *s: jnp.zeros(s, F32)
            chunk = functools.partial(_chunk_scan, n_heads=n_ret, dk=ret_dk, dv=ret_dv)
            o_p, s_p, c_p2, n_p, m_p = chunk(
                *rows_p, dec, lgr, lgc, zeros(b, qw, ret_dv), zeros(b, ret_dv, qw), zeros(b, 1, qw), zeros(b, 1, LANES),
                blocks=[t_real // MB] + list(range(t_real // MB)), first_valid=n_meta,
                n_seq_blk=1)
            rows_s_pad = [_pad_page(r[0], n_seq, t_new, MB) for r in rows_s]
            c0 = state_mlstm_C[lj].transpose(0, 2, 1, 3).reshape(n_seq, ml_dv, qw)
            o_s, s_s, c_s2, n_s, m_s = chunk(
                *rows_s_pad, dec, lgr, lgc, state_ret[lj].reshape(n_seq, qw, ret_dv), c0,
                state_mlstm_n[lj].reshape(n_seq, 1, qw), _pad_cols(state_mlstm_m[lj], LANES)[:, None],
                blocks=[0], first_valid=t_new, n_seq_blk=1)
            o_s = o_s[:, :t_new].reshape(1, n_rows_s, 2 * vw)

            acts_p, acts_s, ws = [o_p], [o_s], [w_out]
            for dst, s_, c_, n_, m_, nb in ((prompt_out, s_p, c_p2, n_p, m_p, b), (sample_out, s_s, c_s2, n_s, m_s, n_seq)):
                dst.setdefault("s", []).append(s_.reshape(nb, n_ret, ret_dk, ret_dv))
                dst.setdefault("C", []).append(c_.reshape(nb, ml_dv, n_ml, ml_dk).transpose(0, 2, 1, 3))
                dst.setdefault("n", []).append(n_.reshape(nb, n_ml, ml_dk))
                dst.setdefault("m", []).append(m_[:, 0, :n_ml])

        if last:
            xp = _post(xp, acts_p, ws, g1, b1, g2, b2, w1, w2, tm=_tile(t_real, 512), alpha=alpha, t_out=t_real)
        else:
            xp = _post(xp, acts_p, ws, g1, b1, g2, b2, w1, w2, tm=tm_p, alpha=alpha)
        xs = _post(xs, acts_s, ws, g1, b1, g2, b2, w1, w2, tm=tm_s, alpha=alpha)

    y_prompt = xp
    y_sample = xs.reshape(n_seq, t_new, d)
    k_shape_p = (b, n_meta + t_real, n_diff, 2, diff_qk)
    v_shape_p = (b, n_meta + t_real, n_diff, diff_v)
    k_shape_s = (n_seq, t_new, n_diff, 2, diff_qk)
    v_shape_s = (n_seq, t_new, n_diff, diff_v)
    return (y_prompt, y_sample,
            jnp.stack(prompt_out["c"]), jnp.stack(prompt_out["kr"]),
            jnp.stack([a.reshape(k_shape_p) for a in prompt_out["dk"]]),
            jnp.stack([a.reshape(v_shape_p) for a in prompt_out["dv"]]),
            jnp.stack(prompt_out["s"]), jnp.stack(prompt_out["C"]), jnp.stack(prompt_out["n"]), jnp.stack(prompt_out["m"]),
            jnp.stack(sample_out["c"]), jnp.stack(sample_out["kr"]),
            jnp.stack([a.reshape(k_shape_s) for a in sample_out["dk"]]),
            jnp.stack([a.reshape(v_shape_s) for a in sample_out["dv"]]),
            jnp.stack(sample_out["s"]), jnp.stack(sample_out["C"]), jnp.stack(sample_out["n"]), jnp.stack(sample_out["m"]))


def _tile_pages(n_pages, pref):
    g = min(pref, n_pages)
    while n_pages % g:
        g -= 1
    return g
```
